```python
import math
import jax
import jax.numpy as jnp
from jax import lax
import numpy as np

D_MODEL = 2048
BATCH = 2
SEQ = 4096
DEPTH = 4
DEC_BATCH = 8
DEC_SEQ = 8
PAST_LEN = 16384
PAGE_SIZE = 128

HEAD_DIM = 128
N_HEADS = D_MODEL // HEAD_DIM
MIX_WIDTH = N_HEADS * HEAD_DIM
A_KV_GROUPS = 4
A_REP = N_HEADS // A_KV_GROUPS
A_KV_WIDTH = A_KV_GROUPS * HEAD_DIM
CMP_BLOCK = 64
SEL_BLOCK = CMP_BLOCK
TOP_N = 16
WINDOW = 512
WIN_Q_BLOCK = 128
SEL_Q_CHUNK = 64
FOX_Q_BLOCK = 128
N_A_LAYERS = (DEPTH + 1) // 2
N_B_LAYERS = DEPTH // 2
A_IN_WIDTH = MIX_WIDTH + 3 * 2 * A_KV_WIDTH + 3 * N_HEADS + MIX_WIDTH
B_IN_WIDTH = 3 * MIX_WIDTH + N_HEADS + MIX_WIDTH
A_SPLITS = [MIX_WIDTH, MIX_WIDTH + 2 * A_KV_WIDTH, MIX_WIDTH + 4 * A_KV_WIDTH, MIX_WIDTH + 6 * A_KV_WIDTH, MIX_WIDTH + 6 * A_KV_WIDTH + 3 * N_HEADS]
B_SPLITS = [MIX_WIDTH, 2 * MIX_WIDTH, 3 * MIX_WIDTH, 3 * MIX_WIDTH + N_HEADS]
RMS_EPS = 1e-6
SCALE = HEAD_DIM ** -0.5
FORCED_SCORE = float(A_REP + 1)

kernel_name = 'nsa_fox_hybrid_decode_step'


def _rms(x, gain):
    xf = x.astype(jnp.float32)
    y = xf * lax.rsqrt(jnp.mean(xf * xf, axis=-1, keepdims=True) + RMS_EPS)
    return (y * gain.astype(jnp.float32)).astype(x.dtype)


def _alibi_slopes():
    s = np.exp2(-8.0 * np.arange(1, N_HEADS + 1) / N_HEADS)
    return jnp.asarray(s, dtype=jnp.float32).reshape(A_KV_GROUPS, A_REP)


def _masked_softmax(s, mask):
    s = jnp.where(mask, s, -jnp.inf)
    m = jnp.max(s, axis=-1, keepdims=True)
    m = jnp.where(jnp.isfinite(m), m, 0.0)
    p = jnp.exp(s - m)
    return p / jnp.maximum(jnp.sum(p, axis=-1, keepdims=True), 1e-30)


def _attend(q, k, v, mask, bias):
    s = jnp.einsum('btgrd,bsgd->bgrts', q, k).astype(jnp.float32) * SCALE + bias
    p = _masked_softmax(s, mask)
    o = jnp.einsum('bgrts,bsgd->btgrd', p.astype(v.dtype), v)
    return o, p


def _kv_rows(a, k_gain):
    B, T = a.shape[0], a.shape[1]
    a = a.reshape(B, T, 2, -1, HEAD_DIM)
    return jnp.stack([_rms(a[:, :, 0], k_gain), a[:, :, 1]], axis=2)


def _paged_rows(pool, layer, page_table):
    rows = pool[layer, page_table]
    return rows.reshape((rows.shape[0], rows.shape[1] * rows.shape[2]) + rows.shape[3:])


def _compress(rows, pe, w1, w2):
    h = jax.nn.gelu(jnp.einsum('bnigd,ide->bnge', rows + pe[None, None, :, None, :], w1))
    return jnp.einsum('bnge,ef->bngf', h, w2)


def _win_attn(q, k, v, q_pos, k_pos, slopes):
    d = q_pos[:, None] - k_pos[None, :]
    mask = (d >= 0) & (d < WINDOW) & (k_pos[None, :] >= 0)
    o, _ = _attend(q, k, v, mask, -slopes[:, :, None, None] * d.astype(jnp.float32))
    return o


def _win_prompt(q, win_rows, slopes):
    B, T = q.shape[0], q.shape[1]
    qb = math.gcd(T, WIN_Q_BLOCK)
    nb = T // qb
    span = WINDOW + qb
    kv = jnp.pad(win_rows, ((0, 0), (WINDOW, 0), (0, 0), (0, 0), (0, 0)))
    idx = jnp.arange(nb, dtype=jnp.int32)[:, None] * qb + jnp.arange(span, dtype=jnp.int32)[None, :]
    kv_b = kv[:, idx]
    q_pos = jnp.arange(T, dtype=jnp.int32).reshape(nb, qb)
    o = jax.vmap(_win_attn, in_axes=(1, 1, 1, 0, 0, None), out_axes=1)(
        q.reshape(B, nb, qb, A_KV_GROUPS, A_REP, HEAD_DIM), kv_b[:, :, :, 0], kv_b[:, :, :, 1], q_pos, idx - WINDOW, slopes)
    return o.reshape(B, T, A_KV_GROUPS, A_REP, HEAD_DIM)


def _nsa_cmp_sel(q, q_pos, cmp_all, fetch, pe, w1, w2, k_gain_cmp, slopes):
    B, T = q.shape[0], q.shape[1]
    L = cmp_all.shape[1]
    n_cmp = L // CMP_BLOCK
    blocks = cmp_all[:, : n_cmp * CMP_BLOCK].reshape(B, n_cmp, CMP_BLOCK, 2, A_KV_GROUPS, HEAD_DIM)
    k_c = _rms(_compress(blocks[:, :, :, 0], pe[0], w1[0], w2[0]), k_gain_cmp)
    v_c = _compress(blocks[:, :, :, 1], pe[1], w1[1], w2[1])
    c_end = (jnp.arange(n_cmp, dtype=jnp.int32) + 1) * CMP_BLOCK - 1
    d_c = q_pos[:, None] - c_end[None, :]
    o_cmp, p_cmp = _attend(q, k_c, v_c, d_c >= 0, -slopes[:, :, None, None] * d_c.astype(jnp.float32))

    n_blk = -(-L // SEL_BLOCK)
    imp = jnp.pad(jnp.sum(p_cmp, axis=2), ((0, 0), (0, 0), (0, 0), (0, n_blk - n_cmp)))
    blk = jnp.arange(n_blk, dtype=jnp.int32)[None, :]
    cur = (q_pos // SEL_BLOCK)[:, None]
    forced = (blk == 0) | (blk == cur) | (blk == cur - 1)
    score = jnp.where(blk <= cur, jnp.where(forced, FORCED_SCORE, imp), -jnp.inf)
    n_sel = min(TOP_N, n_blk)
    top_val, top_idx = lax.top_k(score, n_sel)
    top_ok = jnp.isfinite(top_val)

    chunk = math.gcd(T, SEL_Q_CHUNK)
    n_chunk = T // chunk
    offs = jnp.arange(SEL_BLOCK, dtype=jnp.int32)

    def one_chunk(xs):
        q_c, idx, ok, pos = xs
        rows = fetch(idx)
        d = pos[None, None, :, None, None] - (idx[..., None] * SEL_BLOCK + offs)
        mask = (ok[..., None] & (d >= 0)).reshape(B, A_KV_GROUPS, 1, chunk, n_sel * SEL_BLOCK)
        s = (jnp.einsum('bcgrd,bgcnid->bgrcni', q_c, rows[..., 0, :]).astype(jnp.float32) * SCALE
             - slopes[None, :, :, None, None, None] * d[:, :, None].astype(jnp.float32))
        p = _masked_softmax(s.reshape(B, A_KV_GROUPS, A_REP, chunk, n_sel * SEL_BLOCK), mask)
        v_g = rows[..., 1, :].reshape(B, A_KV_GROUPS, chunk, n_sel * SEL_BLOCK, HEAD_DIM)
        return jnp.einsum('bgrcs,bgcsd->bcgrd', p.astype(v_g.dtype), v_g)

    xs = (jnp.moveaxis(q.reshape(B, n_chunk, chunk, A_KV_GROUPS, A_REP, HEAD_DIM), 1, 0),
          jnp.moveaxis(top_idx.reshape(B, A_KV_GROUPS, n_chunk, chunk, n_sel), 2, 0),
          jnp.moveaxis(top_ok.reshape(B, A_KV_GROUPS, n_chunk, chunk, n_sel), 2, 0),
          q_pos.reshape(n_chunk, chunk))
    o_sel = jnp.moveaxis(lax.map(one_chunk, xs), 0, 1).reshape(B, T, A_KV_GROUPS, A_REP, HEAD_DIM)
    return o_cmp, o_sel


def _nsa_layer(x, norm_g, w_in, q_gain, k_gain, pe, w1, w2, w_out, slopes, past=None):
    B, T = x.shape[0], x.shape[1]
    proj = _rms(x, norm_g) @ w_in
    q, cmp, sel, win, gates, z = jnp.split(proj, A_SPLITS, axis=-1)
    q = _rms(q.reshape(B, T, A_KV_GROUPS, A_REP, HEAD_DIM), q_gain)
    cmp_rows = cmp.reshape(B, T, 2, A_KV_GROUPS, HEAD_DIM)
    sel_rows = _kv_rows(sel, k_gain[1])
    win_rows = _kv_rows(win, k_gain[2])
    b5 = jnp.arange(B)[:, None, None, None, None]
    g5 = jnp.arange(A_KV_GROUPS)[None, :, None, None, None]
    offs = jnp.arange(SEL_BLOCK, dtype=jnp.int32)
    if past is None:
        q_pos = jnp.arange(T, dtype=jnp.int32)
        cmp_all = cmp_rows

        def fetch(idx):
            return sel_rows[b5, idx[..., None] * SEL_BLOCK + offs, :, g5]

        o_win = _win_prompt(q, win_rows, slopes)
        new_win = win_rows[:, T - min(WINDOW, T):]
    else:
        cmp_pool, sel_pool, layer, page_table, win_buf = past
        page_size = cmp_pool.shape[2]
        P = page_table.shape[1] * page_size
        q_pos = P + jnp.arange(T, dtype=jnp.int32)
        cmp_all = jnp.concatenate([_paged_rows(cmp_pool, layer, page_table), cmp_rows], axis=1)
        n_past_blk = P // SEL_BLOCK
        blk_per_page = page_size // SEL_BLOCK

        def fetch(idx):
            p_idx = jnp.clip(idx, 0, n_past_blk - 1)
            page = page_table[b5[..., 0], p_idx // blk_per_page]
            old = sel_pool[layer, page[..., None], (p_idx % blk_per_page)[..., None] * SEL_BLOCK + offs, :, g5]
            new = sel_rows[b5, jnp.clip((idx - n_past_blk)[..., None] * SEL_BLOCK + offs, 0, T - 1), :, g5]
            return jnp.where((idx < n_past_blk)[..., None, None, None], old, new)

        wb = win_buf.shape[1]
        win_all = jnp.concatenate([win_buf, win_rows], axis=1)
        k_pos = P - wb + jnp.arange(wb + T, dtype=jnp.int32)
        o_win = _win_attn(q, win_all[:, :, 0], win_all[:, :, 1], q_pos, k_pos, slopes)
        new_win = win_all[:, T:]
    o_cmp, o_sel = _nsa_cmp_sel(q, q_pos, cmp_all, fetch, pe, w1, w2, k_gain[0], slopes)
    g = jax.nn.sigmoid(gates.astype(jnp.float32)).reshape(B, T, 3, A_KV_GROUPS, A_REP, 1).astype(x.dtype)
    o = g[:, :, 0] * o_cmp + g[:, :, 1] * o_sel + g[:, :, 2] * o_win
    y = x + (o.reshape(B, T, MIX_WIDTH) * jax.nn.silu(z)) @ w_out
    return y, cmp_rows, sel_rows, new_win


def _fox_prompt(q, k, v, cum):
    B, T = q.shape[0], q.shape[1]
    qb = math.gcd(T, FOX_Q_BLOCK)
    nb = T // qb
    k_pos = jnp.arange(T, dtype=jnp.int32)
    cum_k = jnp.transpose(cum, (0, 2, 1))[:, :, None, None, :]

    def one_block(xs):
        q_blk, cum_q, pos = xs
        bias = jnp.transpose(cum_q, (0, 2, 1))[:, :, None, :, None] - cum_k
        o, _ = _attend(q_blk, k, v, k_pos[None, :] <= pos[:, None], bias)
        return o

    xs = (jnp.moveaxis(q.reshape(B, nb, qb, N_HEADS, 1, HEAD_DIM), 1, 0),
          jnp.moveaxis(cum.reshape(B, nb, qb, N_HEADS), 1, 0),
          k_pos.reshape(nb, qb))
    return jnp.moveaxis(lax.map(one_block, xs), 0, 1).reshape(B, T, N_HEADS, 1, HEAD_DIM)


def _fox_layer(x, norm_g, w_in, f_bias, q_gain, k_gain, w_out, past=None):
    B, T = x.shape[0], x.shape[1]
    proj = _rms(x, norm_g) @ w_in
    q, k, v, f_logit, z = jnp.split(proj, B_SPLITS, axis=-1)
    q = _rms(q.reshape(B, T, N_HEADS, 1, HEAD_DIM), q_gain)
    k = _rms(k.reshape(B, T, N_HEADS, HEAD_DIM), k_gain)
    v = v.reshape(B, T, N_HEADS, HEAD_DIM)
    log_f = jax.nn.log_sigmoid((f_logit + f_bias).astype(jnp.float32))
    if past is None:
        o = _fox_prompt(q, k, v, jnp.cumsum(log_f, axis=1))
    else:
        kv_pool, logf_pool, layer, page_table = past
        kv_past = _paged_rows(kv_pool, layer, page_table)
        logf_past = _paged_rows(logf_pool, layer, page_table).astype(jnp.float32)
        P = kv_past.shape[1]
        cum = jnp.transpose(jnp.cumsum(jnp.concatenate([logf_past, log_f], axis=1), axis=1), (0, 2, 1))
        bias = cum[:, :, None, P:, None] - cum[:, :, None, None, :]
        q_pos = P + jnp.arange(T, dtype=jnp.int32)
        k_pos = jnp.arange(P + T, dtype=jnp.int32)
        s = jnp.concatenate([jnp.einsum('btgrd,bsgd->bgrts', q, kv_past[:, :, 0]),
                             jnp.einsum('btgrd,bsgd->bgrts', q, k)], axis=-1).astype(jnp.float32) * SCALE + bias
        p = _masked_softmax(s, k_pos[None, :] <= q_pos[:, None]).astype(v.dtype)
        o = (jnp.einsum('bgrts,bsgd->btgrd', p[..., :P], kv_past[:, :, 1])
             + jnp.einsum('bgrts,bsgd->btgrd', p[..., P:], v))
    y = x + (o.reshape(B, T, MIX_WIDTH) * jax.nn.silu(z)) @ w_out
    return y, jnp.stack([k, v], axis=2), log_f.astype(x.dtype)


def setup_inputs(seed: int = 0) -> dict:
    key = jax.random.key(seed)
    ks = jax.random.split(key, 22)
    f32 = jnp.float32
    n_pages = PAST_LEN // PAGE_SIZE
    n_used = DEC_BATCH * n_pages
    n_pool = n_used + max(1, n_used // 4)
    win_buf = min(WINDOW, PAST_LEN)

    def nrm(k, shape, scale=1.0):
        return scale * jax.random.normal(k, shape, f32)

    return {
        'x_prompt': nrm(ks[0], (BATCH, SEQ, D_MODEL)),
        'x_sample': nrm(ks[1], (DEC_BATCH, DEC_SEQ, D_MODEL)),
        'cache_a_cmp': nrm(ks[2], (N_A_LAYERS, n_pool, PAGE_SIZE, 2, A_KV_GROUPS, HEAD_DIM)),
        'cache_a_sel': nrm(ks[3], (N_A_LAYERS, n_pool, PAGE_SIZE, 2, A_KV_GROUPS, HEAD_DIM)),
        'state_a_win': nrm(ks[4], (N_A_LAYERS, DEC_BATCH, win_buf, 2, A_KV_GROUPS, HEAD_DIM)),
        'cache_b_kv': nrm(ks[5], (N_B_LAYERS, n_pool, PAGE_SIZE, 2, N_HEADS, HEAD_DIM)),
        'cache_b_logf': jax.nn.log_sigmoid(2.0 + nrm(ks[6], (N_B_LAYERS, n_pool, PAGE_SIZE, N_HEADS))),
        'page_table': jax.random.permutation(ks[7], n_pool)[:n_used].reshape(DEC_BATCH, n_pages).astype(jnp.int32),
        'a_norm': 1.0 + nrm(ks[8], (N_A_LAYERS, D_MODEL), 0.02),
        'a_w_in': nrm(ks[9], (N_A_LAYERS, D_MODEL, A_IN_WIDTH), D_MODEL ** -0.5),
        'a_q_gain': 1.0 + nrm(ks[10], (N_A_LAYERS, HEAD_DIM), 0.02),
        'a_k_gain': 1.0 + nrm(ks[11], (N_A_LAYERS, 3, HEAD_DIM), 0.02),
        'a_phi_pe': nrm(ks[12], (N_A_LAYERS, 2, CMP_BLOCK, HEAD_DIM), 0.1),
        'a_phi_w1': nrm(ks[13], (N_A_LAYERS, 2, CMP_BLOCK, HEAD_DIM, HEAD_DIM), (CMP_BLOCK * HEAD_DIM) ** -0.5),
        'a_phi_w2': nrm(ks[14], (N_A_LAYERS, 2, HEAD_DIM, HEAD_DIM), HEAD_DIM ** -0.5),
        'a_w_out': nrm(ks[15], (N_A_LAYERS, MIX_WIDTH, D_MODEL), MIX_WIDTH ** -0.5),
        'b_norm': 1.0 + nrm(ks[16], (N_B_LAYERS, D_MODEL), 0.02),
        'b_w_in': nrm(ks[17], (N_B_LAYERS, D_MODEL, B_IN_WIDTH), D_MODEL ** -0.5),
        'b_f_bias': 2.0 + nrm(ks[18], (N_B_LAYERS, N_HEADS), 0.5),
        'b_q_gain': 1.0 + nrm(ks[19], (N_B_LAYERS, HEAD_DIM), 0.02),
        'b_k_gain': 1.0 + nrm(ks[20], (N_B_LAYERS, HEAD_DIM), 0.02),
        'b_w_out': nrm(ks[21], (N_B_LAYERS, MIX_WIDTH, D_MODEL), MIX_WIDTH ** -0.5),
    }


def reference(x_prompt, x_sample, cache_a_cmp, cache_a_sel, state_a_win, cache_b_kv, cache_b_logf, page_table,
              a_norm, a_w_in, a_q_gain, a_k_gain, a_phi_pe, a_phi_w1, a_phi_w2, a_w_out,
              b_norm, b_w_in, b_f_bias, b_q_gain, b_k_gain, b_w_out):
    slopes = _alibi_slopes()
    xp, xs = x_prompt, x_sample
    a_cmp_p, a_cmp_s, a_sel_p, a_sel_s, a_win_p, a_win_s = [], [], [], [], [], []
    b_kv_p, b_kv_s, b_lf_p, b_lf_s = [], [], [], []
    for i in range(DEPTH):
        j = i // 2
        if i % 2 == 0:
            prm = (a_norm[j], a_w_in[j], a_q_gain[j], a_k_gain[j], a_phi_pe[j], a_phi_w1[j], a_phi_w2[j], a_w_out[j], slopes)
            xp, cmp_p, sel_p, win_p = _nsa_layer(xp, *prm)
            xs, cmp_s, sel_s, win_s = _nsa_layer(xs, *prm, past=(cache_a_cmp, cache_a_sel, j, page_table, state_a_win[j]))
            a_cmp_p.append(cmp_p)
            a_cmp_s.append(cmp_s)
            a_sel_p.append(sel_p)
            a_sel_s.append(sel_s)
            a_win_p.append(win_p)
            a_win_s.append(win_s)
        else:
            prm = (b_norm[j], b_w_in[j], b_f_bias[j], b_q_gain[j], b_k_gain[j], b_w_out[j])
            xp, kv_p, lf_p = _fox_layer(xp, *prm)
            xs, kv_s, lf_s = _fox_layer(xs, *prm, past=(cache_b_kv, cache_b_logf, j, page_table))
            b_kv_p.append(kv_p)
            b_kv_s.append(kv_s)
            b_lf_p.append(lf_p)
            b_lf_s.append(lf_s)
    return (xp, xs,
            jnp.stack(a_cmp_p), jnp.stack(a_cmp_s), jnp.stack(a_sel_p), jnp.stack(a_sel_s),
            jnp.stack(a_win_p), jnp.stack(a_win_s),
            jnp.stack(b_kv_p), jnp.stack(b_kv_s), jnp.stack(b_lf_p), jnp.stack(b_lf_s))
```

```python
import functools
import math

import jax
import jax.numpy as jnp
import numpy as np
from jax import lax
from jax.experimental import pallas as pl
from jax.experimental.pallas import tpu as pltpu

HEAD_DIM = 128
N_HEADS = 16
A_KV_GROUPS = 4
A_REP = N_HEADS // A_KV_GROUPS
CMP_BLOCK = 64
SEL_BLOCK = CMP_BLOCK
TOP_N = 16
WINDOW = 512
WIN_Q_BLOCK = 128
SEL_Q_CHUNK = 64
RMS_EPS = 1e-6
SCALE = HEAD_DIM ** -0.5
FORCED_SCORE = float(A_REP + 1)
NEG_BIG = -1e30

LANES = 128
PROJ_COLS = 512
VMEM_LIMIT = 56 * 1024 * 1024

F32 = jnp.float32
BF16 = jnp.bfloat16


def _cparams(sem):
    return pltpu.CompilerParams(dimension_semantics=sem, vmem_limit_bytes=VMEM_LIMIT)


def _head_rms(a, gain):
    ms = jnp.mean(a * a, axis=-1, keepdims=True)
    return a * lax.rsqrt(ms + RMS_EPS) * gain


def _proj_kernel(plan, small_kind, x_ref, ng_ref, gains_ref, sbias_ref, w_ref, ws_ref, *refs):
    n_out = len(refs) - 1
    outs, xn_ref = refs[:n_out], refs[n_out]
    j = pl.program_id(1)

    @pl.when(j == 0)
    def _():
        x = x_ref[...]
        ms = jnp.mean(x * x, axis=-1, keepdims=True)
        xn_ref[...] = (x * lax.rsqrt(ms + RMS_EPS) * ng_ref[...]).astype(BF16)
        small = jnp.dot(xn_ref[...], ws_ref[...], preferred_element_type=F32)
        if small_kind == "sigmoid":
            small = jax.nn.sigmoid(small)
        else:
            small = small + sbias_ref[...]
            small = jnp.minimum(small, 0.0) - jnp.log1p(jnp.exp(-jnp.abs(small)))
        outs[n_out - 1][...] = small

    for (j0, j1, kind, gain_row, targets) in plan:
        @pl.when((j >= j0) & (j < j1))
        def _(kind=kind, gain_row=gain_row, targets=targets):
            acc = jnp.dot(xn_ref[...], w_ref[...], preferred_element_type=F32)
            if kind == "norm":
                g = gains_ref[gain_row:gain_row + 1, :]
                acc = jnp.concatenate(
                    [_head_rms(acc[:, h * LANES:(h + 1) * LANES], g) for h in range(PROJ_COLS // LANES)], axis=1)
            elif kind == "silu":
                acc = acc * jax.nn.sigmoid(acc)
            for t in targets:
                outs[t][...] = acc.astype(outs[t].dtype)


def _proj(x2d, norm_gain, gains, sbias, w_main, w_small, plan, out_defs, small_kind, bm):
    M, D = x2d.shape
    nblk = w_main.shape[1] // PROJ_COLS
    grid = (M // bm, nblk)

    def out_map(j0, n):
        return lambda i, j: (i, jnp.clip(j - j0, 0, n - 1))

    out_shapes = [jax.ShapeDtypeStruct((M, n * PROJ_COLS), dt) for (_, n, dt) in out_defs]
    out_specs = [pl.BlockSpec((bm, PROJ_COLS), out_map(j0, n)) for (j0, n, _) in out_defs]
    out_shapes.append(jax.ShapeDtypeStruct((M, LANES), F32))
    out_specs.append(pl.BlockSpec((bm, LANES), lambda i, j: (i, 0)))
    return pl.pallas_call(
        functools.partial(_proj_kernel, plan, small_kind),
        grid=grid,
        in_specs=[
            pl.BlockSpec((bm, D), lambda i, j: (i, 0)),
            pl.BlockSpec((1, D), lambda i, j: (0, 0)),
            pl.BlockSpec((8, LANES), lambda i, j: (0, 0)),
            pl.BlockSpec((1, LANES), lambda i, j: (0, 0)),
            pl.BlockSpec((D, PROJ_COLS), lambda i, j: (0, j)),
            pl.BlockSpec((D, LANES), lambda i, j: (0, 0)),
        ],
        out_specs=out_specs,
        out_shape=out_shapes,
        scratch_shapes=[pltpu.VMEM((bm, D), BF16)],
        compiler_params=_cparams(("parallel", "arbitrary")),
        name="proj",
    )(x2d, norm_gain.reshape(1, D), gains, sbias, w_main, w_small)


def _outproj_kernel(u_ref, w_ref, x_ref, o_ref):
    o_ref[...] = x_ref[...] + jnp.dot(u_ref[...], w_ref[...], preferred_element_type=F32)


def _outproj(u2d, w_bf, x2d, bm):
    M, K = u2d.shape
    N = w_bf.shape[1]
    return pl.pallas_call(
        _outproj_kernel,
        grid=(M // bm,),
        in_specs=[
            pl.BlockSpec((bm, K), lambda i: (i, 0)),
            pl.BlockSpec((K, N), lambda i: (0, 0)),
            pl.BlockSpec((bm, N), lambda i: (i, 0)),
        ],
        out_specs=pl.BlockSpec((bm, N), lambda i: (i, 0)),
        out_shape=jax.ShapeDtypeStruct((M, N), F32),
        compiler_params=_cparams(("parallel",)),
        name="outproj",
    )(u2d, w_bf, x2d)


def _compress_kernel(n_blk, x0_ref, x1_ref, x2_ref, x3_ref, pe_ref, w1_ref, w2_ref, kg_ref, o_ref):
    kv = pl.program_id(1)

    def body(i, acc):
        pe = pe_ref[pl.ds(i, 1), :]
        a = jnp.concatenate(
            [x_ref[pl.ds(i, n_blk, stride=CMP_BLOCK), :] + pe for x_ref in (x0_ref, x1_ref, x2_ref, x3_ref)],
            axis=0)
        return acc + jnp.dot(a.astype(BF16), w1_ref[i], preferred_element_type=F32)

    acc = lax.fori_loop(0, CMP_BLOCK, body, jnp.zeros((A_KV_GROUPS * n_blk, HEAD_DIM), F32))
    h = jax.nn.gelu(acc)
    out = jnp.dot(h.astype(BF16), w2_ref[...], preferred_element_type=F32)
    out = jnp.where(kv == 0, _head_rms(out, kg_ref[...]), out)
    for g in range(A_KV_GROUPS):
        o_ref[g] = out[g * n_blk:(g + 1) * n_blk].astype(o_ref.dtype)


def _compress_prompt(rows, pe, w1_bf, w2_bf, k_gain_cmp):
    B, T = rows.shape[0], rows.shape[1]
    n_blk = T // CMP_BLOCK
    G = A_KV_GROUPS
    grp = lambda g: pl.BlockSpec((None, T, HEAD_DIM), lambda b, kv, g=g: (b, 0, kv * G + g))
    return pl.pallas_call(
        functools.partial(_compress_kernel, n_blk),
        grid=(B, 2),
        in_specs=[
            grp(0), grp(1), grp(2), grp(3),
            pl.BlockSpec((None, CMP_BLOCK, HEAD_DIM), lambda b, kv: (kv, 0, 0)),
            pl.BlockSpec((None, CMP_BLOCK, HEAD_DIM, HEAD_DIM), lambda b, kv: (kv, 0, 0, 0)),
            pl.BlockSpec((None, HEAD_DIM, HEAD_DIM), lambda b, kv: (kv, 0, 0)),
            pl.BlockSpec((1, HEAD_DIM), lambda b, kv: (0, 0)),
        ],
        out_specs=pl.BlockSpec((None, None, A_KV_GROUPS, n_blk, HEAD_DIM), lambda b, kv: (b, kv, 0, 0, 0)),
        out_shape=jax.ShapeDtypeStruct((B, 2, A_KV_GROUPS, n_blk, HEAD_DIM), BF16),
        compiler_params=_cparams(("parallel", "arbitrary")),
        name="compress",
    )(rows, rows, rows, rows, pe, w1_bf, w2_bf, k_gain_cmp.reshape(1, HEAD_DIM))


def _softmax_rows(s, ok):
    m = jnp.max(s, axis=-1, keepdims=True)
    p = jnp.where(ok, jnp.exp(s - m), 0.0)
    return p / jnp.maximum(jnp.sum(p, axis=-1, keepdims=True), 1e-30)


def _select_blocks(imp, cur, n_blk):
    blk = lax.broadcasted_iota(jnp.int32, (1, n_blk), 1)
    valid = blk <= cur
    forced = (blk == 0) | (blk == cur) | (blk == cur - 1)
    score = jnp.where(valid, jnp.where(forced, FORCED_SCORE, imp), -jnp.inf)
    rank = jnp.zeros(score.shape, F32)
    for i in range(n_blk):
        col = score[:, i:i + 1]
        rank = rank + jnp.where(blk > i, jnp.where(col >= score, 1.0, 0.0), jnp.where(col > score, 1.0, 0.0))
    return valid & (rank < float(min(TOP_N, n_blk)))


def _nsa_prompt_kernel(tq, tk, T, q_ref, kc_ref, vc_ref, ks_ref, vs_ref, kw_ref, vw_ref, gt_ref, zs_ref,
                       slope_ref, exp_ref, o_ref):
    qi = pl.program_id(2)
    t0 = qi * tq
    n_cmp = T // CMP_BLOCK
    R = A_REP
    q = q_ref[...]
    q4 = jnp.concatenate([q[:, r * HEAD_DIM:(r + 1) * HEAD_DIM] for r in range(R)], axis=0)
    qpos = t0 + lax.broadcasted_iota(jnp.int32, (tq, 1), 0)
    slopes = [slope_ref[r:r + 1, 0:1] for r in range(R)]
    nt = (((1,), (1,)), ((), ()))

    sc = lax.dot_general(q4, kc_ref[...], nt, preferred_element_type=F32) * SCALE
    c_end = (lax.broadcasted_iota(jnp.int32, (1, n_cmp), 1) + 1) * CMP_BLOCK - 1
    ok_c = qpos >= c_end
    cpos = (c_end - t0).astype(F32)
    p_parts = []
    for r in range(R):
        s_r = sc[r * tq:(r + 1) * tq] + slopes[r] * cpos
        p_parts.append(_softmax_rows(jnp.where(ok_c, s_r, NEG_BIG), ok_c))
    p_c = jnp.concatenate(p_parts, axis=0)
    o_cmp = jnp.dot(p_c.astype(BF16), vc_ref[...], preferred_element_type=F32)
    imp = p_parts[0]
    for r in range(1, R):
        imp = imp + p_parts[r]
    sel = _select_blocks(imp, qpos // SEL_BLOCK, n_cmp)
    sel_bf = jnp.where(sel, 1.0, 0.0).astype(BF16)

    def sel_body(kt, carry):
        m, l, acc = carry
        k0 = pl.multiple_of(kt * tk, tk)
        k = ks_ref[pl.ds(k0, tk), :]
        v = vs_ref[pl.ds(k0, tk), :]
        s = lax.dot_general(q4, k, nt, preferred_element_type=F32) * SCALE
        kpos = k0 + lax.broadcasted_iota(jnp.int32, (1, tk), 1)
        picked = jnp.dot(sel_bf, exp_ref[:, pl.ds(k0, tk)], preferred_element_type=F32)
        ok = (picked > 0.5) & (kpos <= qpos)
        kposf = (kpos - t0).astype(F32)
        s_parts = []
        for r in range(R):
            s_parts.append(jnp.where(ok, s[r * tq:(r + 1) * tq] + slopes[r] * kposf, NEG_BIG))
        s = jnp.concatenate(s_parts, axis=0)
        ok4 = jnp.concatenate([ok] * R, axis=0)
        m_new = jnp.maximum(m, jnp.max(s, axis=-1, keepdims=True))
        alpha = jnp.exp(m - m_new)
        p = jnp.where(ok4, jnp.exp(s - m_new), 0.0)
        l = alpha * l + jnp.sum(p, axis=-1, keepdims=True)
        acc = alpha * acc + jnp.dot(p.astype(BF16), v, preferred_element_type=F32)
        return m_new, l, acc

    n_kt = (t0 + tq + tk - 1) // tk
    init = (jnp.full((R * tq, 1), NEG_BIG, F32), jnp.zeros((R * tq, 1), F32), jnp.zeros((R * tq, HEAD_DIM), F32))
    _, l_s, acc_s = lax.fori_loop(0, n_kt, sel_body, init)
    o_sel = acc_s / jnp.maximum(l_s, 1e-30)

    span = WINDOW + tq
    w0 = pl.multiple_of(jnp.maximum(t0 - WINDOW, 0), tq)
    kw = kw_ref[pl.ds(w0, span), :]
    vw = vw_ref[pl.ds(w0, span), :]
    sw = lax.dot_general(q4, kw, nt, preferred_element_type=F32) * SCALE
    wpos = w0 + lax.broadcasted_iota(jnp.int32, (1, span), 1)
    d_w = qpos - wpos
    ok_w = (d_w >= 0) & (d_w < WINDOW)
    wposf = (wpos - t0).astype(F32)
    pw_parts = []
    for r in range(R):
        s_r = sw[r * tq:(r + 1) * tq] + slopes[r] * wposf
        pw_parts.append(_softmax_rows(jnp.where(ok_w, s_r, NEG_BIG), ok_w))
    o_win = jnp.dot(jnp.concatenate(pw_parts, axis=0).astype(BF16), vw, preferred_element_type=F32)

    gt = gt_ref[...]
    zs = zs_ref[...]
    outs = []
    for r in range(R):
        rows = slice(r * tq, (r + 1) * tq)
        o_r = (gt[:, r:r + 1] * o_cmp[rows] + gt[:, R + r:R + r + 1] * o_sel[rows]
               + gt[:, 2 * R + r:2 * R + r + 1] * o_win[rows])
        outs.append(o_r * zs[:, r * HEAD_DIM:(r + 1) * HEAD_DIM].astype(F32))
    o_ref[...] = jnp.concatenate(outs, axis=1).astype(o_ref.dtype)


def _nsa_prompt_attn(q_bf, kvc, rows_bf, gates_g, zs, slope_tab, expand, tq=128, tk=512):
    B, T, W = q_bf.shape
    G = A_KV_GROUPS
    gw = A_REP * HEAD_DIM
    n_cmp = T // CMP_BLOCK
    tk = min(tk, T)
    full = lambda col: pl.BlockSpec((None, T, HEAD_DIM), lambda b, g, i, col=col: (b, 0, col * G + g))
    return pl.pallas_call(
        functools.partial(_nsa_prompt_kernel, tq, tk, T),
        grid=(B, G, T // tq),
        in_specs=[
            pl.BlockSpec((None, tq, gw), lambda b, g, i: (b, i, g)),
            pl.BlockSpec((None, None, None, n_cmp, HEAD_DIM), lambda b, g, i: (b, 0, g, 0, 0)),
            pl.BlockSpec((None, None, None, n_cmp, HEAD_DIM), lambda b, g, i: (b, 1, g, 0, 0)),
            full(0), full(1), full(2), full(3),
            pl.BlockSpec((None, None, tq, 3 * A_REP), lambda b, g, i: (b, g, i, 0)),
            pl.BlockSpec((None, tq, gw), lambda b, g, i: (b, i, g)),
            pl.BlockSpec((None, 8, LANES), lambda b, g, i: (g, 0, 0)),
            pl.BlockSpec((n_cmp, T), lambda b, g, i: (0, 0)),
        ],
        out_specs=pl.BlockSpec((None, tq, gw), lambda b, g, i: (b, i, g)),
        out_shape=jax.ShapeDtypeStruct((B, T, W), BF16),
        compiler_params=_cparams(("parallel", "parallel", "arbitrary")),
        name="nsa_prompt_attn",
    )(q_bf, kvc, kvc, rows_bf, rows_bf, rows_bf, rows_bf, gates_g, zs, slope_tab, expand)


def _split3(x):
    hi = x.astype(BF16)
    r1 = x - hi.astype(F32)
    mid = r1.astype(BF16)
    lo = (r1 - mid.astype(F32)).astype(BF16)
    return hi, mid, lo


def _cumsum_kernel(tb, lf_ref, o_ref, carry_ref):
    @pl.when(pl.program_id(1) == 0)
    def _():
        carry_ref[...] = jnp.zeros_like(carry_ref)

    lower = (lax.broadcasted_iota(jnp.int32, (tb, tb), 0) >= lax.broadcasted_iota(jnp.int32, (tb, tb), 1))
    lower = jnp.where(lower, 1.0, 0.0).astype(BF16)
    acc = jnp.zeros((tb, LANES), F32)
    for part in _split3(lf_ref[...]):
        acc = acc + jnp.dot(lower, part, preferred_element_type=F32)
    acc = acc.T + carry_ref[...]
    o_ref[...] = acc
    carry_ref[...] = acc[:, tb - 1:tb]


def _cumsum_heads(logf, tb=256):
    B, T, _ = logf.shape
    tb = min(tb, T)
    return pl.pallas_call(
        functools.partial(_cumsum_kernel, tb),
        grid=(B, T // tb),
        in_specs=[pl.BlockSpec((None, tb, LANES), lambda b, i: (b, i, 0))],
        out_specs=pl.BlockSpec((None, LANES, tb), lambda b, i: (b, 0, i)),
        out_shape=jax.ShapeDtypeStruct((B, LANES, T), F32),
        scratch_shapes=[pltpu.VMEM((LANES, 1), F32)],
        compiler_params=_cparams(("parallel", "arbitrary")),
        name="fox_cumsum",
    )(logf)


def _fox_prompt_kernel(tq, tk, q_ref, k_ref, v_ref, cum_ref, zs_ref, o_ref):
    qi = pl.program_id(2)
    t0 = qi * tq
    q = q_ref[...]
    qpos = t0 + lax.broadcasted_iota(jnp.int32, (tq, 1), 0)
    nt = (((1,), (1,)), ((), ()))

    def step(kt, carry, masked):
        m, l, acc = carry
        k0 = pl.multiple_of(kt * tk, tk)
        k = k_ref[pl.ds(k0, tk), :]
        v = v_ref[pl.ds(k0, tk), :]
        s = lax.dot_general(q, k, nt, preferred_element_type=F32) * SCALE - cum_ref[:, pl.ds(k0, tk)]
        if masked:
            ok = (k0 + lax.broadcasted_iota(jnp.int32, (1, tk), 1)) <= qpos
            s = jnp.where(ok, s, NEG_BIG)
        m_new = jnp.maximum(m, jnp.max(s, axis=-1, keepdims=True))
        alpha = jnp.exp(m - m_new)
        p = jnp.exp(s - m_new)
        if masked:
            p = jnp.where(ok, p, 0.0)
        l = alpha * l + jnp.sum(p, axis=-1, keepdims=True)
        acc = alpha * acc + jnp.dot(p.astype(BF16), v, preferred_element_type=F32)
        return m_new, l, acc

    init = (jnp.full((tq, 1), NEG_BIG, F32), jnp.zeros((tq, 1), F32), jnp.zeros((tq, HEAD_DIM), F32))
    n_full = t0 // tk
    carry = lax.fori_loop(0, n_full, lambda kt, c: step(kt, c, False), init)
    for d in range(tq // tk):
        carry = step(n_full + d, carry, True)
    _, l, acc = carry
    o_ref[...] = (acc / jnp.maximum(l, 1e-30) * zs_ref[...].astype(F32)).astype(o_ref.dtype)


def _fox_prompt_attn(q_bf, kv_bf, cum, zs, tq=512, tk=512):
    B, T, W = q_bf.shape
    H = N_HEADS
    tq, tk = min(tq, T), min(tk, T)
    return pl.pallas_call(
        functools.partial(_fox_prompt_kernel, tq, tk),
        grid=(B, H, T // tq),
        in_specs=[
            pl.BlockSpec((None, tq, HEAD_DIM), lambda b, h, i: (b, i, h)),
            pl.BlockSpec((None, T, HEAD_DIM), lambda b, h, i: (b, 0, h)),
            pl.BlockSpec((None, T, HEAD_DIM), lambda b, h, i: (b, 0, H + h)),
            pl.BlockSpec((None, None, 1, T), lambda b, h, i: (b, h, 0, 0)),
            pl.BlockSpec((None, tq, HEAD_DIM), lambda b, h, i: (b, i, h)),
        ],
        out_specs=pl.BlockSpec((None, tq, HEAD_DIM), lambda b, h, i: (b, i, h)),
        out_shape=jax.ShapeDtypeStruct((B, T, W), BF16),
        compiler_params=_cparams(("parallel", "parallel", "arbitrary")),
        name="fox_prompt_attn",
    )(q_bf, kv_bf, kv_bf, cum, zs)


def _alibi_slopes():
    s = np.exp2(-8.0 * np.arange(1, N_HEADS + 1) / N_HEADS)
    return jnp.asarray(s, dtype=F32).reshape(A_KV_GROUPS, A_REP)


def _nsa_weights(w_in, q_gain, k_gain):
    mix = N_HEADS * HEAD_DIM
    kvw = A_KV_GROUPS * HEAD_DIM
    g0 = mix + 6 * kvw
    g1 = g0 + 3 * N_HEADS
    w_main = jnp.concatenate([w_in[:, :g0], w_in[:, g1:]], axis=1).astype(BF16)
    w_small = jnp.pad(w_in[:, g0:g1], ((0, 0), (0, LANES - 3 * N_HEADS))).astype(BF16)
    gains = jnp.concatenate([q_gain[None], k_gain, jnp.zeros((4, HEAD_DIM), F32)], axis=0)
    return w_main, w_small, gains


_NSA_PLAN = (
    (0, 4, "norm", 0, (0,)),
    (4, 6, "raw", 0, (1,)),
    (6, 7, "norm", 2, (1, 2)),
    (7, 8, "raw", 0, (1, 2)),
    (8, 9, "norm", 3, (1, 2)),
    (9, 10, "raw", 0, (1, 2)),
    (10, 14, "silu", 0, (3,)),
)
_NSA_OUTS = ((0, 4, BF16), (4, 6, F32), (6, 4, BF16), (10, 4, BF16))


def _nsa_proj(x2d, norm_g, w_main, w_small, gains, bm):
    zero_bias = jnp.zeros((1, LANES), F32)
    return _proj(x2d, norm_g, gains, zero_bias, w_main, w_small, _NSA_PLAN, _NSA_OUTS, "sigmoid", bm)


def _nsa_prompt_layer(x, norm_g, w_in, q_gain, k_gain, pe, w1, w2, w_out, slopes):
    B, T, D = x.shape
    M = B * T
    w_main, w_small, gains = _nsa_weights(w_in, q_gain, k_gain)
    x2d = x.reshape(M, D)
    q_bf, rows, rows_bf, zs, gates = _nsa_proj(x2d, norm_g, w_main, w_small, gains, 512)
    rows3 = rows.reshape(B, T, -1)
    kvc = _compress_prompt(rows3, pe, w1.astype(BF16), w2.astype(BF16), k_gain[0])
    gates_g = gates[:, :3 * N_HEADS].reshape(B, T, 3, A_KV_GROUPS, A_REP)
    gates_g = jnp.transpose(gates_g, (0, 3, 1, 2, 4)).reshape(B, A_KV_GROUPS, T, 3 * A_REP)
    slope_tab = jnp.broadcast_to(
        jnp.pad(slopes, ((0, 0), (0, 8 - A_REP)))[:, :, None], (A_KV_GROUPS, 8, LANES))
    n_cmp = T // CMP_BLOCK
    expand = (jnp.arange(T, dtype=jnp.int32)[None, :] // SEL_BLOCK
              == jnp.arange(n_cmp, dtype=jnp.int32)[:, None]).astype(BF16)
    u = _nsa_prompt_attn(q_bf.reshape(B, T, -1), kvc, rows_bf.reshape(B, T, -1), gates_g,
                         zs.reshape(B, T, -1), slope_tab, expand)
    y = _outproj(u.reshape(M, -1), w_out.astype(BF16), x2d, 512).reshape(B, T, D)
    kvw = 2 * A_KV_GROUPS * HEAD_DIM
    shp = (B, -1, 2, A_KV_GROUPS, HEAD_DIM)
    cmp_rows = rows3[:, :, :kvw].reshape(shp)
    sel_rows = rows3[:, :, kvw:2 * kvw].reshape(shp)
    new_win = rows3[:, T - min(WINDOW, T):, 2 * kvw:].reshape(shp)
    return y, cmp_rows, sel_rows, new_win


def _fox_weights(w_in, q_gain, k_gain, f_bias):
    mix = N_HEADS * HEAD_DIM
    w_main = jnp.concatenate([w_in[:, :3 * mix], w_in[:, 3 * mix + N_HEADS:]], axis=1).astype(BF16)
    w_small = jnp.pad(w_in[:, 3 * mix:3 * mix + N_HEADS], ((0, 0), (0, LANES - N_HEADS))).astype(BF16)
    gains = jnp.concatenate([q_gain[None], k_gain[None], jnp.zeros((6, HEAD_DIM), F32)], axis=0)
    sbias = jnp.pad(f_bias, (0, LANES - N_HEADS)).reshape(1, LANES)
    return w_main, w_small, gains, sbias


_FOX_PLAN = (
    (0, 4, "norm", 0, (0,)),
    (4, 8, "norm", 1, (1, 2)),
    (8, 12, "raw", 0, (1, 2)),
    (12, 16, "silu", 0, (3,)),
)
_FOX_OUTS = ((0, 4, BF16), (4, 8, F32), (4, 8, BF16), (12, 4, BF16))


def _fox_prompt_layer(x, norm_g, w_in, f_bias, q_gain, k_gain, w_out):
    B, T, D = x.shape
    M = B * T
    w_main, w_small, gains, sbias = _fox_weights(w_in, q_gain, k_gain, f_bias)
    x2d = x.reshape(M, D)
    q_bf, kv, kv_bf, zs, logf = _proj(x2d, norm_g, gains, sbias, w_main, w_small, _FOX_PLAN, _FOX_OUTS,
                                      "logsig", 512)
    cum = _cumsum_heads(logf.reshape(B, T, LANES))[:, :N_HEADS].reshape(B, N_HEADS, 1, T)
    u = _fox_prompt_attn(q_bf.reshape(B, T, -1), kv_bf.reshape(B, T, -1), cum, zs.reshape(B, T, -1))
    y = _outproj(u.reshape(M, -1), w_out.astype(BF16), x2d, 512).reshape(B, T, D)
    return y, kv.reshape(B, T, 2, N_HEADS, HEAD_DIM), logf[:, :N_HEADS].reshape(B, T, N_HEADS)


def _rms(x, gain):
    xf = x.astype(F32)
    y = xf * lax.rsqrt(jnp.mean(xf * xf, axis=-1, keepdims=True) + RMS_EPS)
    return (y * gain.astype(F32)).astype(x.dtype)


def _masked_softmax(s, mask):
    s = jnp.where(mask, s, -jnp.inf)
    m = jnp.max(s, axis=-1, keepdims=True)
    m = jnp.where(jnp.isfinite(m), m, 0.0)
    p = jnp.exp(s - m)
    return p / jnp.maximum(jnp.sum(p, axis=-1, keepdims=True), 1e-30)


def _attend(q, k, v, mask, bias):
    s = jnp.einsum('btgrd,bsgd->bgrts', q, k).astype(F32) * SCALE + bias
    p = _masked_softmax(s, mask)
    o = jnp.einsum('bgrts,bsgd->btgrd', p.astype(v.dtype), v)
    return o, p


def _kv_rows(a, k_gain):
    B, T = a.shape[0], a.shape[1]
    a = a.reshape(B, T, 2, -1, HEAD_DIM)
    return jnp.stack([_rms(a[:, :, 0], k_gain), a[:, :, 1]], axis=2)


def _paged_rows(pool, layer, page_table):
    rows = pool[layer, page_table]
    return rows.reshape((rows.shape[0], rows.shape[1] * rows.shape[2]) + rows.shape[3:])


def _compress(rows, pe, w1, w2):
    h = jax.nn.gelu(jnp.einsum('bnigd,ide->bnge', rows + pe[None, None, :, None, :], w1))
    return jnp.einsum('bnge,ef->bngf', h, w2)


def _win_attn(q, k, v, q_pos, k_pos, slopes):
    d = q_pos[:, None] - k_pos[None, :]
    mask = (d >= 0) & (d < WINDOW) & (k_pos[None, :] >= 0)
    o, _ = _attend(q, k, v, mask, -slopes[:, :, None, None] * d.astype(F32))
    return o


def _nsa_cmp_sel(q, q_pos, cmp_all, fetch, pe, w1, w2, k_gain_cmp, slopes):
    B, T = q.shape[0], q.shape[1]
    L = cmp_all.shape[1]
    n_cmp = L // CMP_BLOCK
    blocks = cmp_all[:, : n_cmp * CMP_BLOCK].reshape(B, n_cmp, CMP_BLOCK, 2, A_KV_GROUPS, HEAD_DIM)
    k_c = _rms(_compress(blocks[:, :, :, 0], pe[0], w1[0], w2[0]), k_gain_cmp)
    v_c = _compress(blocks[:, :, :, 1], pe[1], w1[1], w2[1])
    c_end = (jnp.arange(n_cmp, dtype=jnp.int32) + 1) * CMP_BLOCK - 1
    d_c = q_pos[:, None] - c_end[None, :]
    o_cmp, p_cmp = _attend(q, k_c, v_c, d_c >= 0, -slopes[:, :, None, None] * d_c.astype(F32))
    n_blk = -(-L // SEL_BLOCK)
    imp = jnp.pad(jnp.sum(p_cmp, axis=2), ((0, 0), (0, 0), (0, 0), (0, n_blk - n_cmp)))
    blk = jnp.arange(n_blk, dtype=jnp.int32)[None, :]
    cur = (q_pos // SEL_BLOCK)[:, None]
    forced = (blk == 0) | (blk == cur) | (blk == cur - 1)
    score = jnp.where(blk <= cur, jnp.where(forced, FORCED_SCORE, imp), -jnp.inf)
    n_sel = min(TOP_N, n_blk)
    top_val, top_idx = lax.top_k(score, n_sel)
    top_ok = jnp.isfinite(top_val)
    chunk = math.gcd(T, SEL_Q_CHUNK)
    n_chunk = T // chunk
    offs = jnp.arange(SEL_BLOCK, dtype=jnp.int32)

    def one_chunk(xs):
        q_c, idx, ok, pos = xs
        rows = fetch(idx)
        d = pos[None, None, :, None, None] - (idx[..., None] * SEL_BLOCK + offs)
        mask = (ok[..., None] & (d >= 0)).reshape(B, A_KV_GROUPS, 1, chunk, n_sel * SEL_BLOCK)
        s = (jnp.einsum('bcgrd,bgcnid->bgrcni', q_c, rows[..., 0, :]).astype(F32) * SCALE
             - slopes[None, :, :, None, None, None] * d[:, :, None].astype(F32))
        p = _masked_softmax(s.reshape(B, A_KV_GROUPS, A_REP, chunk, n_sel * SEL_BLOCK), mask)
        v_g = rows[..., 1, :].reshape(B, A_KV_GROUPS, chunk, n_sel * SEL_BLOCK, HEAD_DIM)
        return jnp.einsum('bgrcs,bgcsd->bcgrd', p.astype(v_g.dtype), v_g)

    xs = (jnp.moveaxis(q.reshape(B, n_chunk, chunk, A_KV_GROUPS, A_REP, HEAD_DIM), 1, 0),
          jnp.moveaxis(top_idx.reshape(B, A_KV_GROUPS, n_chunk, chunk, n_sel), 2, 0),
          jnp.moveaxis(top_ok.reshape(B, A_KV_GROUPS, n_chunk, chunk, n_sel), 2, 0),
          q_pos.reshape(n_chunk, chunk))
    o_sel = jnp.moveaxis(lax.map(one_chunk, xs), 0, 1).reshape(B, T, A_KV_GROUPS, A_REP, HEAD_DIM)
    return o_cmp, o_sel


def _nsa_sample_layer(x, norm_g, w_in, q_gain, k_gain, pe, w1, w2, w_out, slopes, past):
    B, T, D = x.shape
    M = B * T
    mix = N_HEADS * HEAD_DIM
    kvw = A_KV_GROUPS * HEAD_DIM
    splits = [mix, mix + 2 * kvw, mix + 4 * kvw, mix + 6 * kvw, mix + 6 * kvw + 3 * N_HEADS]
    proj = _rms(x, norm_g) @ w_in
    q, cmp, sel, win, gates, z = jnp.split(proj, splits, axis=-1)
    q = _rms(q.reshape(B, T, A_KV_GROUPS, A_REP, HEAD_DIM), q_gain)
    cmp_rows = cmp.reshape(B, T, 2, A_KV_GROUPS, HEAD_DIM)
    sel_rows = _kv_rows(sel, k_gain[1])
    win_rows = _kv_rows(win, k_gain[2])
    b5 = jnp.arange(B)[:, None, None, None, None]
    g5 = jnp.arange(A_KV_GROUPS)[None, :, None, None, None]
    offs = jnp.arange(SEL_BLOCK, dtype=jnp.int32)
    cmp_pool, sel_pool, layer, page_table, win_buf = past
    page_size = cmp_pool.shape[2]
    P = page_table.shape[1] * page_size
    q_pos = P + jnp.arange(T, dtype=jnp.int32)
    cmp_all = jnp.concatenate([_paged_rows(cmp_pool, layer, page_table), cmp_rows], axis=1)
    n_past_blk = P // SEL_BLOCK
    blk_per_page = page_size // SEL_BLOCK

    def fetch(idx):
        p_idx = jnp.clip(idx, 0, n_past_blk - 1)
        page = page_table[b5[..., 0], p_idx // blk_per_page]
        old = sel_pool[layer, page[..., None], (p_idx % blk_per_page)[..., None] * SEL_BLOCK + offs, :, g5]
        new = sel_rows[b5, jnp.clip((idx - n_past_blk)[..., None] * SEL_BLOCK + offs, 0, T - 1), :, g5]
        return jnp.where((idx < n_past_blk)[..., None, None, None], old, new)

    wb = win_buf.shape[1]
    win_all = jnp.concatenate([win_buf, win_rows], axis=1)
    k_pos = P - wb + jnp.arange(wb + T, dtype=jnp.int32)
    o_win = _win_attn(q, win_all[:, :, 0], win_all[:, :, 1], q_pos, k_pos, slopes)
    new_win = win_all[:, T:]
    o_cmp, o_sel = _nsa_cmp_sel(q, q_pos, cmp_all, fetch, pe, w1, w2, k_gain[0], slopes)
    g = jax.nn.sigmoid(gates.astype(F32)).reshape(B, T, 3, A_KV_GROUPS, A_REP, 1).astype(x.dtype)
    o = g[:, :, 0] * o_cmp + g[:, :, 1] * o_sel + g[:, :, 2] * o_win
    u = (o.reshape(B, T, mix) * jax.nn.silu(z)).astype(BF16)
    y = _outproj(u.reshape(M, mix), w_out.astype(BF16), x.reshape(M, D), M).reshape(B, T, D)
    return y, cmp_rows, sel_rows, new_win


def _fox_sample_layer(x, norm_g, w_in, f_bias, q_gain, k_gain, w_out, past):
    B, T, D = x.shape
    M = B * T
    mix = N_HEADS * HEAD_DIM
    proj = _rms(x, norm_g) @ w_in
    q, k, v, f_logit, z = jnp.split(proj, [mix, 2 * mix, 3 * mix, 3 * mix + N_HEADS], axis=-1)
    q = _rms(q.reshape(B, T, N_HEADS, 1, HEAD_DIM), q_gain)
    k = _rms(k.reshape(B, T, N_HEADS, HEAD_DIM), k_gain)
    v = v.reshape(B, T, N_HEADS, HEAD_DIM)
    log_f = jax.nn.log_sigmoid((f_logit + f_bias).astype(F32))
    kv_pool, logf_pool, layer, page_table = past
    kv_past = _paged_rows(kv_pool, layer, page_table)
    logf_past = _paged_rows(logf_pool, layer, page_table).astype(F32)
    P = kv_past.shape[1]
    cum = jnp.transpose(jnp.cumsum(jnp.concatenate([logf_past, log_f], axis=1), axis=1), (0, 2, 1))
    bias = cum[:, :, None, P:, None] - cum[:, :, None, None, :]
    q_pos = P + jnp.arange(T, dtype=jnp.int32)
    k_pos = jnp.arange(P + T, dtype=jnp.int32)
    s = jnp.concatenate([jnp.einsum('btgrd,bsgd->bgrts', q, kv_past[:, :, 0]),
                         jnp.einsum('btgrd,bsgd->bgrts', q, k)], axis=-1).astype(F32) * SCALE + bias
    p = _masked_softmax(s, k_pos[None, :] <= q_pos[:, None]).astype(v.dtype)
    o = (jnp.einsum('bgrts,bsgd->btgrd', p[..., :P], kv_past[:, :, 1])
         + jnp.einsum('bgrts,bsgd->btgrd', p[..., P:], v))
    u = (o.reshape(B, T, mix) * jax.nn.silu(z)).astype(BF16)
    y = _outproj(u.reshape(M, mix), w_out.astype(BF16), x.reshape(M, D), M).reshape(B, T, D)
    return y, jnp.stack([k, v], axis=2), log_f.astype(x.dtype)


def kernel(x_prompt, x_sample, cache_a_cmp, cache_a_sel, state_a_win, cache_b_kv, cache_b_logf, page_table,
           a_norm, a_w_in, a_q_gain, a_k_gain, a_phi_pe, a_phi_w1, a_phi_w2, a_w_out,
           b_norm, b_w_in, b_f_bias, b_q_gain, b_k_gain, b_w_out):
    slopes = _alibi_slopes()
    xp, xs = x_prompt, x_sample
    depth = a_norm.shape[0] + b_norm.shape[0]
    a_cmp_p, a_cmp_s, a_sel_p, a_sel_s, a_win_p, a_win_s = [], [], [], [], [], []
    b_kv_p, b_kv_s, b_lf_p, b_lf_s = [], [], [], []
    for i in range(depth):
        j = i // 2
        if i % 2 == 0:
            prm = (a_norm[j], a_w_in[j], a_q_gain[j], a_k_gain[j], a_phi_pe[j], a_phi_w1[j], a_phi_w2[j],
                   a_w_out[j], slopes)
            xp, cmp_p, sel_p, win_p = _nsa_prompt_layer(xp, *prm)
            xs, cmp_s, sel_s, win_s = _nsa_sample_layer(
                xs, *prm, past=(cache_a_cmp, cache_a_sel, j, page_table, state_a_win[j]))
            a_cmp_p.append(cmp_p)
            a_cmp_s.append(cmp_s)
            a_sel_p.append(sel_p)
            a_sel_s.append(sel_s)
            a_win_p.append(win_p)
            a_win_s.append(win_s)
        else:
            prm = (b_norm[j], b_w_in[j], b_f_bias[j], b_q_gain[j], b_k_gain[j], b_w_out[j])
            xp, kv_p, lf_p = _fox_prompt_layer(xp, *prm)
            xs, kv_s, lf_s = _fox_sample_layer(xs, *prm, past=(cache_b_kv, cache_b_logf, j, page_table))
            b_kv_p.append(kv_p)
            b_kv_s.append(kv_s)
            b_lf_p.append(lf_p)
            b_lf_s.append(lf_s)
    return (xp, xs,
            jnp.stack(a_cmp_p), jnp.stack(a_cmp_s), jnp.stack(a_sel_p), jnp.stack(a_sel_s),
            jnp.stack(a_win_p), jnp.stack(a_win_s),
            jnp.stack(b_kv_p), jnp.stack(b_kv_s), jnp.stack(b_lf_p), jnp.stack(b_lf_s))
```

```python
import functools

import jax
import jax.numpy as jnp
import numpy as np
from jax import lax
from jax.experimental import pallas as pl
from jax.experimental.pallas import tpu as pltpu

HEAD_DIM = 128
N_HEADS = 16
A_KV_GROUPS = 4
A_REP = N_HEADS // A_KV_GROUPS
CMP_BLOCK = 64
SEL_BLOCK = CMP_BLOCK
TOP_N = 16
WINDOW = 512
RMS_EPS = 1e-6
SCALE = HEAD_DIM ** -0.5
FORCED_SCORE = float(A_REP + 1)
NEG_BIG = -1e30

LANES = 128
SUBLANES = 8
PROJ_COLS = 512
VMEM_LIMIT = 56 * 1024 * 1024
CMP_PAGES = 16
SEL_PAGES = 8
FOX_PAGES = 4
LOGF_PAGES = 8

F32 = jnp.float32
BF16 = jnp.bfloat16
NT = (((1,), (1,)), ((), ()))
TN = (((0,), (0,)), ((), ()))


def _cparams(sem):
    return pltpu.CompilerParams(dimension_semantics=sem, vmem_limit_bytes=VMEM_LIMIT)


def _head_rms(a, gain):
    ms = jnp.mean(a * a, axis=-1, keepdims=True)
    return a * lax.rsqrt(ms + RMS_EPS) * gain


def _softmax_rows(s, ok):
    m = jnp.max(s, axis=-1, keepdims=True)
    p = jnp.where(ok, jnp.exp(s - m), 0.0)
    return p / jnp.maximum(jnp.sum(p, axis=-1, keepdims=True), 1e-30)


def _split3(x):
    hi = x.astype(BF16)
    r1 = x - hi.astype(F32)
    mid = r1.astype(BF16)
    lo = (r1 - mid.astype(F32)).astype(BF16)
    return hi, mid, lo


def _lower_ones(n):
    tri = lax.broadcasted_iota(jnp.int32, (n, n), 0) >= lax.broadcasted_iota(jnp.int32, (n, n), 1)
    return jnp.where(tri, 1.0, 0.0).astype(BF16)


def _rank_counts(score, blk, n_blk):
    rank = jnp.zeros(score.shape, F32)
    for i in range(n_blk):
        col = score[:, i:i + 1]
        rank = rank + jnp.where(blk > i, jnp.where(col >= score, 1.0, 0.0), jnp.where(col > score, 1.0, 0.0))
    return rank


def _proj_kernel(plan, small_kind, x_ref, ng_ref, gains_ref, sbias_ref, w_ref, ws_ref, *refs):
    n_out = len(refs) - 1
    outs, xn_ref = refs[:n_out], refs[n_out]
    j = pl.program_id(1)

    @pl.when(j == 0)
    def _():
        x = x_ref[...]
        ms = jnp.mean(x * x, axis=-1, keepdims=True)
        xn_ref[...] = (x * lax.rsqrt(ms + RMS_EPS) * ng_ref[...]).astype(BF16)
        small = jnp.dot(xn_ref[...], ws_ref[...], preferred_element_type=F32)
        if small_kind == "sigmoid":
            small = jax.nn.sigmoid(small)
        else:
            small = small + sbias_ref[...]
            small = jnp.minimum(small, 0.0) - jnp.log1p(jnp.exp(-jnp.abs(small)))
        outs[n_out - 1][...] = small

    for (j0, j1, kind, gain_row, targets) in plan:
        @pl.when((j >= j0) & (j < j1))
        def _(kind=kind, gain_row=gain_row, targets=targets):
            acc = jnp.dot(xn_ref[...], w_ref[...], preferred_element_type=F32)
            if kind in ("norm", "normq"):
                g = gains_ref[gain_row:gain_row + 1, :]
                if kind == "normq":
                    g = g * SCALE
                acc = jnp.concatenate(
                    [_head_rms(acc[:, h * LANES:(h + 1) * LANES], g) for h in range(PROJ_COLS // LANES)], axis=1)
            elif kind == "silu":
                acc = acc * jax.nn.sigmoid(acc)
            for t in targets:
                outs[t][...] = acc.astype(outs[t].dtype)


def _proj(x2d, norm_gain, gains, sbias, w_main, w_small, plan, out_defs, small_kind, bm):
    M, D = x2d.shape
    nblk = w_main.shape[1] // PROJ_COLS
    grid = (M // bm, nblk)

    def out_map(j0, n):
        return lambda i, j: (i, jnp.clip(j - j0, 0, n - 1))

    out_shapes = [jax.ShapeDtypeStruct((M, n * PROJ_COLS), dt) for (_, n, dt) in out_defs]
    out_specs = [pl.BlockSpec((bm, PROJ_COLS), out_map(j0, n)) for (j0, n, _) in out_defs]
    out_shapes.append(jax.ShapeDtypeStruct((M, LANES), F32))
    out_specs.append(pl.BlockSpec((bm, LANES), lambda i, j: (i, 0)))
    return pl.pallas_call(
        functools.partial(_proj_kernel, plan, small_kind),
        grid=grid,
        in_specs=[
            pl.BlockSpec((bm, D), lambda i, j: (i, 0)),
            pl.BlockSpec((1, D), lambda i, j: (0, 0)),
            pl.BlockSpec((8, LANES), lambda i, j: (0, 0)),
            pl.BlockSpec((1, LANES), lambda i, j: (0, 0)),
            pl.BlockSpec((D, PROJ_COLS), lambda i, j: (0, j)),
            pl.BlockSpec((D, LANES), lambda i, j: (0, 0)),
        ],
        out_specs=out_specs,
        out_shape=out_shapes,
        scratch_shapes=[pltpu.VMEM((bm, D), BF16)],
        compiler_params=_cparams(("parallel", "arbitrary")),
        name="proj",
    )(x2d, norm_gain.reshape(1, D), gains, sbias, w_main, w_small)


def _outproj_kernel(u_ref, w_ref, x_ref, o_ref):
    o_ref[...] = x_ref[...] + jnp.dot(u_ref[...].astype(BF16), w_ref[...], preferred_element_type=F32)


def _outproj(u2d, w_bf, x2d, bm):
    M, K = u2d.shape
    N = w_bf.shape[1]
    return pl.pallas_call(
        _outproj_kernel,
        grid=(M // bm,),
        in_specs=[
            pl.BlockSpec((bm, K), lambda i: (i, 0)),
            pl.BlockSpec((K, N), lambda i: (0, 0)),
            pl.BlockSpec((bm, N), lambda i: (i, 0)),
        ],
        out_specs=pl.BlockSpec((bm, N), lambda i: (i, 0)),
        out_shape=jax.ShapeDtypeStruct((M, N), F32),
        compiler_params=_cparams(("parallel",)),
        name="outproj",
    )(u2d, w_bf, x2d)


def _compress_kernel(n_blk, x0_ref, x1_ref, x2_ref, x3_ref, pe_ref, w1_ref, w2_ref, kg_ref, o_ref):
    kv = pl.program_id(1)

    def body(i, acc):
        pe = pe_ref[pl.ds(i, 1), :]
        a = jnp.concatenate(
            [x_ref[pl.ds(i, n_blk, stride=CMP_BLOCK), :] + pe for x_ref in (x0_ref, x1_ref, x2_ref, x3_ref)],
            axis=0)
        return acc + jnp.dot(a.astype(BF16), w1_ref[i], preferred_element_type=F32)

    acc = lax.fori_loop(0, CMP_BLOCK, body, jnp.zeros((A_KV_GROUPS * n_blk, HEAD_DIM), F32))
    h = jax.nn.gelu(acc)
    out = jnp.dot(h.astype(BF16), w2_ref[...], preferred_element_type=F32)
    out = jnp.where(kv == 0, _head_rms(out, kg_ref[...]), out)
    for g in range(A_KV_GROUPS):
        o_ref[g] = out[g * n_blk:(g + 1) * n_blk].astype(o_ref.dtype)


def _compress_prompt(rows, pe, w1_bf, w2_bf, k_gain_cmp):
    B, T = rows.shape[0], rows.shape[1]
    n_blk = T // CMP_BLOCK
    G = A_KV_GROUPS
    grp = lambda g: pl.BlockSpec((None, T, HEAD_DIM), lambda b, kv, g=g: (b, 0, kv * G + g))
    return pl.pallas_call(
        functools.partial(_compress_kernel, n_blk),
        grid=(B, 2),
        in_specs=[
            grp(0), grp(1), grp(2), grp(3),
            pl.BlockSpec((None, CMP_BLOCK, HEAD_DIM), lambda b, kv: (kv, 0, 0)),
            pl.BlockSpec((None, CMP_BLOCK, HEAD_DIM, HEAD_DIM), lambda b, kv: (kv, 0, 0, 0)),
            pl.BlockSpec((None, HEAD_DIM, HEAD_DIM), lambda b, kv: (kv, 0, 0)),
            pl.BlockSpec((1, HEAD_DIM), lambda b, kv: (0, 0)),
        ],
        out_specs=pl.BlockSpec((None, None, A_KV_GROUPS, n_blk, HEAD_DIM), lambda b, kv: (b, kv, 0, 0, 0)),
        out_shape=jax.ShapeDtypeStruct((B, 2, A_KV_GROUPS, n_blk, HEAD_DIM), BF16),
        compiler_params=_cparams(("parallel", "arbitrary")),
        name="compress",
    )(rows, rows, rows, rows, pe, w1_bf, w2_bf, k_gain_cmp.reshape(1, HEAD_DIM))


def _compress_paged_kernel(n_pg, bpp, pt_ref, *refs):
    pages = refs[:n_pg]
    pe_ref, w1_ref, w2_ref, kg_ref, o_ref = refs[n_pg:]
    n_rows = n_pg * bpp * SUBLANES

    def body(i, acc):
        pe = pe_ref[i]
        a = jnp.concatenate([pg[h * CMP_BLOCK + i] + pe for pg in pages for h in range(bpp)], axis=0)
        return acc + jnp.dot(a.astype(BF16), w1_ref[i], preferred_element_type=F32)

    acc = lax.fori_loop(0, CMP_BLOCK, body, jnp.zeros((n_rows, 2 * HEAD_DIM), F32))
    h = jax.nn.gelu(acc)
    out_k = jnp.dot(h[:, :HEAD_DIM].astype(BF16), w2_ref[0], preferred_element_type=F32)
    out_v = jnp.dot(h[:, HEAD_DIM:].astype(BF16), w2_ref[1], preferred_element_type=F32)
    out_k = _head_rms(out_k, kg_ref[...])
    is_k = (lax.broadcasted_iota(jnp.int32, (n_rows, 1), 0) & (SUBLANES - 1)) < A_KV_GROUPS
    o_ref[...] = jnp.where(is_k, out_k, out_v)


def _compress_paged(pool, layer, page_table, pe, w1, w2, k_gain_cmp):
    page = pool.shape[2]
    bpp = page // CMP_BLOCK
    B, n_pages = page_table.shape
    n_pg = CMP_PAGES
    assert page % CMP_BLOCK == 0 and n_pages % n_pg == 0 and 2 * A_KV_GROUPS == SUBLANES
    pool5 = pool.reshape(pool.shape[0], pool.shape[1], page, SUBLANES, HEAD_DIM)
    pe8 = jnp.repeat(jnp.transpose(pe, (1, 0, 2)), A_KV_GROUPS, axis=1)
    w1cat = jnp.concatenate([w1[0], w1[1]], axis=-1).astype(BF16)
    rows_per_step = n_pg * bpp * SUBLANES
    page_spec = lambda p: pl.BlockSpec(
        (None, None, page, SUBLANES, HEAD_DIM), lambda b, c, pt, p=p: (layer, pt[b, c * n_pg + p], 0, 0, 0))
    return pl.pallas_call(
        functools.partial(_compress_paged_kernel, n_pg, bpp),
        grid_spec=pltpu.PrefetchScalarGridSpec(
            num_scalar_prefetch=1,
            grid=(B, n_pages // n_pg),
            in_specs=[page_spec(p) for p in range(n_pg)] + [
                pl.BlockSpec((CMP_BLOCK, SUBLANES, HEAD_DIM), lambda b, c, pt: (0, 0, 0)),
                pl.BlockSpec((CMP_BLOCK, HEAD_DIM, 2 * HEAD_DIM), lambda b, c, pt: (0, 0, 0)),
                pl.BlockSpec((2, HEAD_DIM, HEAD_DIM), lambda b, c, pt: (0, 0, 0)),
                pl.BlockSpec((1, HEAD_DIM), lambda b, c, pt: (0, 0)),
            ],
            out_specs=pl.BlockSpec((None, rows_per_step, HEAD_DIM), lambda b, c, pt: (b, c, 0)),
        ),
        out_shape=jax.ShapeDtypeStruct((B, n_pages * bpp * SUBLANES, HEAD_DIM), F32),
        compiler_params=_cparams(("parallel", "arbitrary")),
        name="compress_paged",
    )(page_table, *([pool5] * n_pg), pe8, w1cat, w2.astype(BF16), k_gain_cmp.reshape(1, HEAD_DIM))


def _select_blocks_t(imp_t, cur, n_blk):
    blk = lax.broadcasted_iota(jnp.int32, (n_blk, 1), 0)
    valid = blk <= cur
    forced = (blk == 0) | (blk == cur) | (blk == cur - 1)
    score = jnp.where(valid, jnp.where(forced, FORCED_SCORE, imp_t), -jnp.inf)
    sub = lax.broadcasted_iota(jnp.int32, (SUBLANES, 1), 0)
    ranks = []
    for b0 in range(0, n_blk, SUBLANES):
        slab = score[b0:b0 + SUBLANES]
        rank = jnp.zeros(slab.shape, F32)
        for i in range(n_blk):
            row = score[i:i + 1]
            if i < b0:
                rank = rank + jnp.where(row >= slab, 1.0, 0.0)
            elif i >= b0 + SUBLANES:
                rank = rank + jnp.where(row > slab, 1.0, 0.0)
            else:
                rank = rank + jnp.where(sub > i - b0, jnp.where(row >= slab, 1.0, 0.0),
                                        jnp.where(row > slab, 1.0, 0.0))
        ranks.append(rank)
    rank = jnp.concatenate(ranks, axis=0)
    return jnp.where(valid & (rank < float(min(TOP_N, n_blk))), 1.0, 0.0)


def _nsa_prompt_kernel(tq, tk, T, q_ref, kc_ref, vc_ref, ks_ref, vs_ref, kw_ref, vw_ref, gt_ref, zs_ref,
                       slope_ref, exp_ref, o_ref):
    qi = pl.program_id(2)
    t0 = qi * tq
    n_cmp = T // CMP_BLOCK
    R = A_REP
    q = q_ref[...]
    q4 = jnp.concatenate([q[:, r * HEAD_DIM:(r + 1) * HEAD_DIM] for r in range(R)], axis=0)
    qpos = t0 + lax.broadcasted_iota(jnp.int32, (tq, 1), 0)
    qpos_row = t0 + lax.broadcasted_iota(jnp.int32, (1, tq), 1)
    slopes = [slope_ref[r:r + 1, 0:1] for r in range(R)]

    sct = lax.dot_general(kc_ref[...], q4, NT, preferred_element_type=F32)
    c_end = (lax.broadcasted_iota(jnp.int32, (n_cmp, 1), 0) + 1) * CMP_BLOCK - 1
    ok_c = qpos_row >= c_end
    cposf = (c_end - t0).astype(F32)
    p_parts = []
    for r in range(R):
        s_r = jnp.where(ok_c, sct[:, r * tq:(r + 1) * tq] + slopes[r] * cposf, NEG_BIG)
        p_r = jnp.where(ok_c, jnp.exp(s_r - jnp.max(s_r, axis=0, keepdims=True)), 0.0)
        p_parts.append(p_r / jnp.maximum(jnp.sum(p_r, axis=0, keepdims=True), 1e-30))
    p_ct = jnp.concatenate(p_parts, axis=1).astype(BF16)
    o_cmp = lax.dot_general(p_ct, vc_ref[...], TN, preferred_element_type=F32)
    imp_t = p_parts[0]
    for r in range(1, R):
        imp_t = imp_t + p_parts[r]
    sel_t = _select_blocks_t(imp_t, qpos_row // SEL_BLOCK, n_cmp).astype(BF16)

    def scores(kt):
        k0 = pl.multiple_of(kt * tk, tk)
        return lax.dot_general(q4, ks_ref[pl.ds(k0, tk), :], NT, preferred_element_type=F32)

    def sel_tile(kt, s, carry, diagonal):
        m, l, acc = carry
        k0 = pl.multiple_of(kt * tk, tk)
        v = vs_ref[pl.ds(k0, tk), :]
        kpos = k0 + lax.broadcasted_iota(jnp.int32, (1, tk), 1)
        picked = lax.dot_general(sel_t, exp_ref[:, pl.ds(k0, tk)], TN, preferred_element_type=F32)
        mask = (picked - 1.0) * (-NEG_BIG)
        if diagonal:
            mask = jnp.where(kpos <= qpos, mask, NEG_BIG)
        kposf = (kpos - t0).astype(F32)
        s = jnp.concatenate(
            [s[r * tq:(r + 1) * tq] + (mask + slopes[r] * kposf) for r in range(R)], axis=0)
        m_new = jnp.maximum(m, jnp.max(s, axis=-1, keepdims=True))
        alpha = jnp.exp(m - m_new)
        p = jnp.exp(s - m_new)
        l = alpha * l + jnp.sum(p, axis=-1, keepdims=True)
        acc = alpha * acc + jnp.dot(p.astype(BF16), v, preferred_element_type=F32)
        return m_new, l, acc

    n_full = t0 // tk
    init = (jnp.full((R * tq, 1), NEG_BIG, F32), jnp.zeros((R * tq, 1), F32), jnp.zeros((R * tq, HEAD_DIM), F32))
    carry = lax.fori_loop(0, n_full, lambda kt, c: sel_tile(kt, scores(kt), c, False), init)
    _, l_s, acc_s = sel_tile(n_full, scores(n_full), carry, True)
    o_sel = acc_s / l_s

    span = WINDOW + tq
    w0 = pl.multiple_of(jnp.maximum(t0 - WINDOW, 0), tq)
    kw = kw_ref[pl.ds(w0, span), :]
    vw = vw_ref[pl.ds(w0, span), :]
    sw = lax.dot_general(q4, kw, NT, preferred_element_type=F32)
    wpos = w0 + lax.broadcasted_iota(jnp.int32, (1, span), 1)
    d_w = qpos - wpos
    mask_w = jnp.where((d_w >= 0) & (d_w < WINDOW), 0.0, NEG_BIG)
    wposf = (wpos - t0).astype(F32)
    sw = jnp.concatenate(
        [sw[r * tq:(r + 1) * tq] + (mask_w + slopes[r] * wposf) for r in range(R)], axis=0)
    pw = jnp.exp(sw - jnp.max(sw, axis=-1, keepdims=True))
    o_win = (jnp.dot(pw.astype(BF16), vw, preferred_element_type=F32)
             / jnp.sum(pw, axis=-1, keepdims=True))

    gt = gt_ref[...]
    zs = zs_ref[...]
    outs = []
    for r in range(R):
        rows = slice(r * tq, (r + 1) * tq)
        o_r = (gt[:, r:r + 1] * o_cmp[rows] + gt[:, R + r:R + r + 1] * o_sel[rows]
               + gt[:, 2 * R + r:2 * R + r + 1] * o_win[rows])
        outs.append(o_r * zs[:, r * HEAD_DIM:(r + 1) * HEAD_DIM].astype(F32))
    o_ref[...] = jnp.concatenate(outs, axis=1).astype(o_ref.dtype)


def _nsa_prompt_attn(q_bf, kvc, rows_bf, gates_g, zs, slope_tab, expand, tq=128, tk=512):
    B, T, W = q_bf.shape
    G = A_KV_GROUPS
    gw = A_REP * HEAD_DIM
    n_cmp = T // CMP_BLOCK
    tk = min(tk, T)
    assert T % tq == 0 and T % tk == 0 and T >= WINDOW + tq
    full = lambda col: pl.BlockSpec((None, T, HEAD_DIM), lambda b, g, i, col=col: (b, 0, col * G + g))
    return pl.pallas_call(
        functools.partial(_nsa_prompt_kernel, tq, tk, T),
        grid=(B, G, T // tq),
        in_specs=[
            pl.BlockSpec((None, tq, gw), lambda b, g, i: (b, i, g)),
            pl.BlockSpec((None, None, None, n_cmp, HEAD_DIM), lambda b, g, i: (b, 0, g, 0, 0)),
            pl.BlockSpec((None, None, None, n_cmp, HEAD_DIM), lambda b, g, i: (b, 1, g, 0, 0)),
            full(0), full(1), full(2), full(3),
            pl.BlockSpec((None, None, tq, 3 * A_REP), lambda b, g, i: (b, g, i, 0)),
            pl.BlockSpec((None, tq, gw), lambda b, g, i: (b, i, g)),
            pl.BlockSpec((None, 8, LANES), lambda b, g, i: (g, 0, 0)),
            pl.BlockSpec((n_cmp, T), lambda b, g, i: (0, 0)),
        ],
        out_specs=pl.BlockSpec((None, tq, gw), lambda b, g, i: (b, i, g)),
        out_shape=jax.ShapeDtypeStruct((B, T, W), BF16),
        compiler_params=_cparams(("parallel", "parallel", "arbitrary")),
        name="nsa_prompt_attn",
    )(q_bf, kvc, kvc, rows_bf, rows_bf, rows_bf, rows_bf, gates_g, zs, slope_tab, expand)


def _nsa_sample_kernel(n_pg, page, P_len, T, wb, pt_ref, q_ref, kvc_ref, *refs):
    pages = refs[:n_pg]
    rows_ref, win_ref, gt_ref, zs_ref, slope_ref, o_ref, m_ref, l_ref, acc_ref, sel_ref, ocmp_ref = refs[n_pg:]
    c = pl.program_id(1)
    G, R = A_KV_GROUPS, A_REP
    n_blk = P_len // SEL_BLOCK
    q = q_ref[...]
    tpos = lax.broadcasted_iota(jnp.int32, (T, 1), 0)

    def q4(g):
        return jnp.concatenate(
            [q[:, (g * R + r) * HEAD_DIM:(g * R + r + 1) * HEAD_DIM] for r in range(R)], axis=0).astype(BF16)

    def slope(g, r):
        return slope_ref[g, r:r + 1, 0:1]

    def flash_update(g, k, v, ok, kposf):
        s = lax.dot_general(q4(g), k, NT, preferred_element_type=F32)
        s = jnp.concatenate(
            [jnp.where(ok, s[r * T:(r + 1) * T] + slope(g, r) * kposf, NEG_BIG) for r in range(R)], axis=0)
        ok4 = jnp.concatenate([ok] * R, axis=0)
        m = m_ref[g][:, 0:1]
        m_new = jnp.maximum(m, jnp.max(s, axis=-1, keepdims=True))
        alpha = jnp.exp(m - m_new)
        p = jnp.where(ok4, jnp.exp(s - m_new), 0.0)
        l_new = alpha * l_ref[g][:, 0:1] + jnp.sum(p, axis=-1, keepdims=True)
        acc_ref[g] = alpha * acc_ref[g] + jnp.dot(p.astype(BF16), v, preferred_element_type=F32)
        m_ref[g] = jnp.broadcast_to(m_new, (R * T, LANES))
        l_ref[g] = jnp.broadcast_to(l_new, (R * T, LANES))

    @pl.when(c == 0)
    def _():
        c_end = (lax.broadcasted_iota(jnp.int32, (1, n_blk), 1) + 1) * CMP_BLOCK - 1
        cposf = (c_end - P_len).astype(F32)
        ok_c = (P_len + tpos) >= c_end
        imps = []
        for g in range(G):
            kc = kvc_ref[pl.ds(g, n_blk, stride=SUBLANES), :].astype(BF16)
            vc = kvc_ref[pl.ds(G + g, n_blk, stride=SUBLANES), :].astype(BF16)
            sc = lax.dot_general(q4(g), kc, NT, preferred_element_type=F32)
            parts = []
            for r in range(R):
                s_r = sc[r * T:(r + 1) * T] + slope(g, r) * cposf
                parts.append(_softmax_rows(jnp.where(ok_c, s_r, NEG_BIG), ok_c))
            ocmp_ref[g] = jnp.dot(jnp.concatenate(parts, axis=0).astype(BF16), vc, preferred_element_type=F32)
            imps.append(parts[0] + parts[1] + parts[2] + parts[3])
        imp = jnp.concatenate(imps, axis=0)
        blk = lax.broadcasted_iota(jnp.int32, (1, n_blk), 1)
        forced = (blk == 0) | (blk == n_blk - 1)
        score = jnp.where(forced, FORCED_SCORE, imp)
        rank = _rank_counts(score, blk, n_blk) + jnp.where(FORCED_SCORE > score, 1.0, 0.0)
        sel_ref[...] = jnp.where(rank < float(TOP_N), 1.0, 0.0)
        m_ref[...] = jnp.full(m_ref.shape, NEG_BIG, F32)
        l_ref[...] = jnp.zeros(l_ref.shape, F32)
        acc_ref[...] = jnp.zeros(acc_ref.shape, F32)

    n_keys = n_pg * page
    kpos = c * n_keys + lax.broadcasted_iota(jnp.int32, (1, n_keys), 1)
    kposf = (kpos - P_len).astype(F32)
    rowblk = lax.broadcasted_iota(jnp.int32, (n_blk, 1), 0)
    expand = jnp.where(rowblk == kpos // SEL_BLOCK, 1.0, 0.0).astype(BF16)
    picked = jnp.dot(sel_ref[...].astype(BF16), expand, preferred_element_type=F32)
    for g in range(G):
        k = jnp.concatenate([pg[pl.ds(g, page, stride=SUBLANES), :] for pg in pages], axis=0).astype(BF16)
        v = jnp.concatenate([pg[pl.ds(G + g, page, stride=SUBLANES), :] for pg in pages], axis=0).astype(BF16)
        flash_update(g, k, v, picked[g * T:(g + 1) * T] > 0.5, kposf)

    @pl.when(c == pl.num_programs(1) - 1)
    def _():
        kvw = G * HEAD_DIM
        rows = rows_ref[...]
        gt = gt_ref[...]
        zs = zs_ref[...]
        pad = jnp.zeros((LANES - T, HEAD_DIM), F32)
        lane = lax.broadcasted_iota(jnp.int32, (1, LANES), 1)
        widx = lax.broadcasted_iota(jnp.int32, (1, wb + LANES), 1)
        ok_w = (widx > tpos + (wb - WINDOW)) & (widx <= tpos + wb)
        wposf = (widx - wb).astype(F32)
        outs = []
        for g in range(G):
            new = lambda sec: jnp.concatenate(
                [rows[:, sec * kvw + g * HEAD_DIM:sec * kvw + (g + 1) * HEAD_DIM], pad], axis=0)
            flash_update(g, new(2).astype(BF16), new(3).astype(BF16), lane <= tpos, lane.astype(F32))
            o_sel = acc_ref[g] / jnp.maximum(l_ref[g][:, 0:1], 1e-30)
            kw = jnp.concatenate([win_ref[pl.ds(g, wb, stride=SUBLANES), :], new(4)], axis=0).astype(BF16)
            vw = jnp.concatenate([win_ref[pl.ds(G + g, wb, stride=SUBLANES), :], new(5)], axis=0).astype(BF16)
            sw = lax.dot_general(q4(g), kw, NT, preferred_element_type=F32)
            parts = []
            for r in range(R):
                s_r = sw[r * T:(r + 1) * T] + slope(g, r) * wposf
                parts.append(_softmax_rows(jnp.where(ok_w, s_r, NEG_BIG), ok_w))
            o_win = jnp.dot(jnp.concatenate(parts, axis=0).astype(BF16), vw, preferred_element_type=F32)
            o_cmp = ocmp_ref[g]
            for r in range(R):
                h = g * R + r
                rs = slice(r * T, (r + 1) * T)
                o_r = (gt[:, h:h + 1] * o_cmp[rs] + gt[:, N_HEADS + h:N_HEADS + h + 1] * o_sel[rs]
                       + gt[:, 2 * N_HEADS + h:2 * N_HEADS + h + 1] * o_win[rs])
                outs.append(o_r * zs[:, h * HEAD_DIM:(h + 1) * HEAD_DIM])
        o_ref[...] = jnp.concatenate(outs, axis=1)


def _nsa_sample_attn(q, kvc, sel_pool, layer, page_table, rows, win_buf, gates, zs, slope_tab, B, T):
    page = sel_pool.shape[2]
    n_pages = page_table.shape[1]
    P_len = n_pages * page
    wb = win_buf.shape[1]
    n_pg = SEL_PAGES
    n_blk = P_len // SEL_BLOCK
    G, R = A_KV_GROUPS, A_REP
    W = N_HEADS * HEAD_DIM
    assert P_len % SEL_BLOCK == 0 and T <= SEL_BLOCK and T % SUBLANES == 0 and n_blk + 1 > TOP_N
    assert n_pages % n_pg == 0 and WINDOW <= wb <= P_len and T <= LANES
    pool4 = sel_pool.reshape(sel_pool.shape[0], sel_pool.shape[1], page * SUBLANES, HEAD_DIM)
    win3 = win_buf.reshape(B, wb * SUBLANES, HEAD_DIM)
    page_spec = lambda p: pl.BlockSpec(
        (None, None, page * SUBLANES, HEAD_DIM), lambda b, c, pt, p=p: (layer, pt[b, c * n_pg + p], 0, 0))
    tok = lambda width: pl.BlockSpec((T, width), lambda b, c, pt: (b, 0))
    return pl.pallas_call(
        functools.partial(_nsa_sample_kernel, n_pg, page, P_len, T, wb),
        grid_spec=pltpu.PrefetchScalarGridSpec(
            num_scalar_prefetch=1,
            grid=(B, n_pages // n_pg),
            in_specs=[tok(W), pl.BlockSpec((None, n_blk * SUBLANES, HEAD_DIM), lambda b, c, pt: (b, 0, 0))]
            + [page_spec(p) for p in range(n_pg)] + [
                tok(rows.shape[1]),
                pl.BlockSpec((None, wb * SUBLANES, HEAD_DIM), lambda b, c, pt: (b, 0, 0)),
                tok(LANES), tok(W),
                pl.BlockSpec((G, 8, LANES), lambda b, c, pt: (0, 0, 0)),
            ],
            out_specs=tok(W),
            scratch_shapes=[
                pltpu.VMEM((G, R * T, LANES), F32), pltpu.VMEM((G, R * T, LANES), F32),
                pltpu.VMEM((G, R * T, HEAD_DIM), F32), pltpu.VMEM((G * T, n_blk), F32),
                pltpu.VMEM((G, R * T, HEAD_DIM), F32),
            ],
        ),
        out_shape=jax.ShapeDtypeStruct((B * T, W), F32),
        compiler_params=_cparams(("parallel", "arbitrary")),
        name="nsa_sample_attn",
    )(page_table, q, kvc, *([pool4] * n_pg), rows, win3, gates, zs, slope_tab)


def _cumsum_kernel(tb, lf_ref, o_ref, carry_ref):
    @pl.when(pl.program_id(1) == 0)
    def _():
        carry_ref[...] = jnp.zeros_like(carry_ref)

    lower = _lower_ones(tb)
    acc = jnp.zeros((tb, LANES), F32)
    for part in _split3(lf_ref[...]):
        acc = acc + jnp.dot(lower, part, preferred_element_type=F32)
    acc = acc.T + carry_ref[...]
    o_ref[...] = acc
    carry_ref[...] = acc[:, tb - 1:tb]


def _cumsum_heads(logf, tb=256):
    B, T, _ = logf.shape
    tb = min(tb, T)
    return pl.pallas_call(
        functools.partial(_cumsum_kernel, tb),
        grid=(B, T // tb),
        in_specs=[pl.BlockSpec((None, tb, LANES), lambda b, i: (b, i, 0))],
        out_specs=pl.BlockSpec((None, LANES, tb), lambda b, i: (b, 0, i)),
        out_shape=jax.ShapeDtypeStruct((B, LANES, T), F32),
        scratch_shapes=[pltpu.VMEM((LANES, 1), F32)],
        compiler_params=_cparams(("parallel", "arbitrary")),
        name="fox_cumsum",
    )(logf)


def _cumsum_paged_kernel(n_pg, page, H, pt_ref, *refs):
    pages = refs[:n_pg]
    o_ref, carry_ref = refs[n_pg:]

    @pl.when(pl.program_id(1) == 0)
    def _():
        carry_ref[...] = jnp.zeros_like(carry_ref)

    lower = _lower_ones(page)
    widen = (lax.broadcasted_iota(jnp.int32, (H, LANES), 0) == lax.broadcasted_iota(jnp.int32, (H, LANES), 1))
    widen = jnp.where(widen, 1.0, 0.0).astype(BF16)
    carry = carry_ref[...]
    for p, pg in enumerate(pages):
        acc = jnp.zeros((page, LANES), F32)
        for part in _split3(pg[...]):
            wide = jnp.dot(part, widen, preferred_element_type=F32).astype(BF16)
            acc = acc + jnp.dot(lower, wide, preferred_element_type=F32)
        cum = acc.T + carry
        o_ref[:, p * page:(p + 1) * page] = cum
        carry = cum[:, page - 1:page]
    carry_ref[...] = carry


def _cumsum_paged(pool, layer, page_table):
    page, H = pool.shape[2], pool.shape[3]
    B, n_pages = page_table.shape
    n_pg = LOGF_PAGES
    assert n_pages % n_pg == 0 and page % LANES == 0
    page_spec = lambda p: pl.BlockSpec(
        (None, None, page, H), lambda b, c, pt, p=p: (layer, pt[b, c * n_pg + p], 0, 0))
    return pl.pallas_call(
        functools.partial(_cumsum_paged_kernel, n_pg, page, H),
        grid_spec=pltpu.PrefetchScalarGridSpec(
            num_scalar_prefetch=1,
            grid=(B, n_pages // n_pg),
            in_specs=[page_spec(p) for p in range(n_pg)],
            out_specs=pl.BlockSpec((None, LANES, n_pg * page), lambda b, c, pt: (b, 0, c)),
            scratch_shapes=[pltpu.VMEM((LANES, 1), F32)],
        ),
        out_shape=jax.ShapeDtypeStruct((B, LANES, n_pages * page), F32),
        compiler_params=_cparams(("parallel", "arbitrary")),
        name="fox_cumsum_paged",
    )(page_table, *([pool] * n_pg))


def _fox_prompt_kernel(tq, tk, q_ref, k_ref, v_ref, cum_ref, zs_ref, o_ref):
    qi = pl.program_id(2)
    t0 = qi * tq
    q = q_ref[...]
    qpos = t0 + lax.broadcasted_iota(jnp.int32, (tq, 1), 0)

    def scores(kt):
        k0 = pl.multiple_of(kt * tk, tk)
        return lax.dot_general(q, k_ref[pl.ds(k0, tk), :], NT, preferred_element_type=F32)

    def step(kt, s, carry, masked):
        m, l, acc = carry
        k0 = pl.multiple_of(kt * tk, tk)
        v = v_ref[pl.ds(k0, tk), :]
        s = s - cum_ref[:, pl.ds(k0, tk)]
        if masked:
            ok = (k0 + lax.broadcasted_iota(jnp.int32, (1, tk), 1)) <= qpos
            s = jnp.where(ok, s, NEG_BIG)
        m_new = jnp.maximum(m, jnp.max(s, axis=-1, keepdims=True))
        alpha = jnp.exp(m - m_new)
        p = jnp.exp(s - m_new)
        if masked:
            p = jnp.where(ok, p, 0.0)
        l = alpha * l + jnp.sum(p, axis=-1, keepdims=True)
        acc = alpha * acc + jnp.dot(p.astype(BF16), v, preferred_element_type=F32)
        return m_new, l, acc

    init = (jnp.full((tq, 1), NEG_BIG, F32), jnp.zeros((tq, 1), F32), jnp.zeros((tq, HEAD_DIM), F32))
    n_full = t0 // tk
    carry = lax.fori_loop(0, n_full, lambda kt, c: step(kt, scores(kt), c, False), init)
    for d in range(tq // tk):
        carry = step(n_full + d, scores(n_full + d), carry, True)
    _, l, acc = carry
    o_ref[...] = (acc / jnp.maximum(l, 1e-30) * zs_ref[...].astype(F32)).astype(o_ref.dtype)


def _fox_prompt_attn(q_bf, kv_bf, cum, zs, tq=512, tk=512):
    B, T, W = q_bf.shape
    H = N_HEADS
    tq, tk = min(tq, T), min(tk, T)
    assert T % tq == 0 and tq % tk == 0
    return pl.pallas_call(
        functools.partial(_fox_prompt_kernel, tq, tk),
        grid=(B, H, T // tq),
        in_specs=[
            pl.BlockSpec((None, tq, HEAD_DIM), lambda b, h, i: (b, i, h)),
            pl.BlockSpec((None, T, HEAD_DIM), lambda b, h, i: (b, 0, h)),
            pl.BlockSpec((None, T, HEAD_DIM), lambda b, h, i: (b, 0, H + h)),
            pl.BlockSpec((None, None, 1, T), lambda b, h, i: (b, h, 0, 0)),
            pl.BlockSpec((None, tq, HEAD_DIM), lambda b, h, i: (b, i, h)),
        ],
        out_specs=pl.BlockSpec((None, tq, HEAD_DIM), lambda b, h, i: (b, i, h)),
        out_shape=jax.ShapeDtypeStruct((B, T, W), BF16),
        compiler_params=_cparams(("parallel", "parallel", "arbitrary")),
        name="fox_prompt_attn",
    )(q_bf, kv_bf, kv_bf, cum, zs)


def _fox_sample_kernel(n_pg, page, T, pt_ref, q_ref, *refs):
    pages = refs[:n_pg]
    cum_ref, kvn_ref, lfn_ref, zs_ref, o_ref, m_ref, l_ref, acc_ref = refs[n_pg:]
    c = pl.program_id(1)
    H = N_HEADS
    q = q_ref[...].astype(BF16)

    @pl.when(c == 0)
    def _():
        m_ref[...] = jnp.full(m_ref.shape, NEG_BIG, F32)
        l_ref[...] = jnp.zeros(l_ref.shape, F32)
        acc_ref[...] = jnp.zeros(acc_ref.shape, F32)

    def update(h, k, v, cum_row, ok):
        s = lax.dot_general(q[:, h * HEAD_DIM:(h + 1) * HEAD_DIM], k, NT, preferred_element_type=F32) - cum_row
        if ok is not None:
            s = jnp.where(ok, s, NEG_BIG)
        m = m_ref[h][:, 0:1]
        m_new = jnp.maximum(m, jnp.max(s, axis=-1, keepdims=True))
        alpha = jnp.exp(m - m_new)
        p = jnp.exp(s - m_new)
        if ok is not None:
            p = jnp.where(ok, p, 0.0)
        l_new = alpha * l_ref[h][:, 0:1] + jnp.sum(p, axis=-1, keepdims=True)
        acc_ref[h] = alpha * acc_ref[h] + jnp.dot(p.astype(BF16), v, preferred_element_type=F32)
        m_ref[h] = jnp.broadcast_to(m_new, (T, LANES))
        l_ref[h] = jnp.broadcast_to(l_new, (T, LANES))

    for h in range(H):
        k = jnp.concatenate([pg[pl.ds(h, page, stride=2 * H), :] for pg in pages], axis=0).astype(BF16)
        v = jnp.concatenate([pg[pl.ds(H + h, page, stride=2 * H), :] for pg in pages], axis=0).astype(BF16)
        update(h, k, v, cum_ref[h:h + 1, :], None)

    @pl.when(c == pl.num_programs(1) - 1)
    def _():
        total = cum_ref[:, n_pg * page - 1:n_pg * page]
        pad = jnp.zeros((LANES - T, LANES), F32)
        acc = jnp.zeros((LANES, LANES), F32)
        lower = _lower_ones(LANES)
        for part in _split3(jnp.concatenate([lfn_ref[...], pad], axis=0)):
            acc = acc + jnp.dot(lower, part, preferred_element_type=F32)
        cum_new = acc.T[:H] + total
        kvn = kvn_ref[...]
        zs = zs_ref[...]
        ok_n = lax.broadcasted_iota(jnp.int32, (1, LANES), 1) <= lax.broadcasted_iota(jnp.int32, (T, 1), 0)
        outs = []
        for h in range(H):
            k_new = jnp.concatenate([kvn[:, h * HEAD_DIM:(h + 1) * HEAD_DIM], pad], axis=0).astype(BF16)
            v_new = jnp.concatenate([kvn[:, (H + h) * HEAD_DIM:(H + h + 1) * HEAD_DIM], pad], axis=0).astype(BF16)
            update(h, k_new, v_new, cum_new[h:h + 1, :], ok_n)
            o = acc_ref[h] / jnp.maximum(l_ref[h][:, 0:1], 1e-30)
            outs.append(o * zs[:, h * HEAD_DIM:(h + 1) * HEAD_DIM])
        o_ref[...] = jnp.concatenate(outs, axis=1)


def _fox_sample_attn(q, kv_pool, layer, page_table, cum, kv_new, logf_new, zs, B, T):
    page = kv_pool.shape[2]
    n_pages = page_table.shape[1]
    n_pg = FOX_PAGES
    H = N_HEADS
    W = H * HEAD_DIM
    assert n_pages % n_pg == 0 and T % SUBLANES == 0 and T <= LANES
    pool4 = kv_pool.reshape(kv_pool.shape[0], kv_pool.shape[1], page * 2 * H, HEAD_DIM)
    page_spec = lambda p: pl.BlockSpec(
        (None, None, page * 2 * H, HEAD_DIM), lambda b, c, pt, p=p: (layer, pt[b, c * n_pg + p], 0, 0))
    tok = lambda width: pl.BlockSpec((T, width), lambda b, c, pt: (b, 0))
    return pl.pallas_call(
        functools.partial(_fox_sample_kernel, n_pg, page, T),
        grid_spec=pltpu.PrefetchScalarGridSpec(
            num_scalar_prefetch=1,
            grid=(B, n_pages // n_pg),
            in_specs=[tok(W)] + [page_spec(p) for p in range(n_pg)] + [
                pl.BlockSpec((None, H, n_pg * page), lambda b, c, pt: (b, 0, c)),
                tok(2 * W), tok(LANES), tok(W),
            ],
            out_specs=tok(W),
            scratch_shapes=[
                pltpu.VMEM((H, T, LANES), F32), pltpu.VMEM((H, T, LANES), F32), pltpu.VMEM((H, T, HEAD_DIM), F32),
            ],
        ),
        out_shape=jax.ShapeDtypeStruct((B * T, W), F32),
        compiler_params=_cparams(("parallel", "arbitrary")),
        name="fox_sample_attn",
    )(page_table, q, *([pool4] * n_pg), cum, kv_new, logf_new, zs)


def _alibi_slopes():
    s = np.exp2(-8.0 * np.arange(1, N_HEADS + 1) / N_HEADS)
    return jnp.asarray(s, dtype=F32).reshape(A_KV_GROUPS, A_REP)


_NSA_PLAN = (
    (0, 4, "normq", 0, (0,)),
    (4, 6, "raw", 0, (1,)),
    (6, 7, "norm", 2, (1, 2)),
    (7, 8, "raw", 0, (1, 2)),
    (8, 9, "norm", 3, (1, 2)),
    (9, 10, "raw", 0, (1, 2)),
    (10, 14, "silu", 0, (3,)),
)


def _nsa_outs(act_dtype):
    return ((0, 4, act_dtype), (4, 6, F32), (6, 4, act_dtype), (10, 4, act_dtype))


def _nsa_layer(xp, xs, norm_g, w_in, q_gain, k_gain, pe, w1, w2, w_out, slopes, past):
    mix = N_HEADS * HEAD_DIM
    kvw = A_KV_GROUPS * HEAD_DIM
    g0 = mix + 6 * kvw
    g1 = g0 + 3 * N_HEADS
    w_main = jnp.concatenate([w_in[:, :g0], w_in[:, g1:]], axis=1).astype(BF16)
    w_small = jnp.pad(w_in[:, g0:g1], ((0, 0), (0, LANES - 3 * N_HEADS))).astype(BF16)
    gains = jnp.concatenate([q_gain[None], k_gain, jnp.zeros((4, HEAD_DIM), F32)], axis=0)
    zero_bias = jnp.zeros((1, LANES), F32)
    w_out_bf = w_out.astype(BF16)
    slope_tab = jnp.broadcast_to(
        jnp.pad(slopes, ((0, 0), (0, 8 - A_REP)))[:, :, None], (A_KV_GROUPS, 8, LANES))
    shp = lambda B: (B, -1, 2, A_KV_GROUPS, HEAD_DIM)

    B, T, D = xp.shape
    M = B * T
    x2d = xp.reshape(M, D)
    q_bf, rows, rows_bf, zs, gates = _proj(x2d, norm_g, gains, zero_bias, w_main, w_small, _NSA_PLAN,
                                           _nsa_outs(BF16), "sigmoid", 512)
    rows3 = rows.reshape(B, T, -1)
    kvc = _compress_prompt(rows3, pe, w1.astype(BF16), w2.astype(BF16), k_gain[0])
    gates_g = gates[:, :3 * N_HEADS].reshape(B, T, 3, A_KV_GROUPS, A_REP)
    gates_g = jnp.transpose(gates_g, (0, 3, 1, 2, 4)).reshape(B, A_KV_GROUPS, T, 3 * A_REP)
    n_cmp = T // CMP_BLOCK
    expand = (jnp.arange(T, dtype=jnp.int32)[None, :] // SEL_BLOCK
              == jnp.arange(n_cmp, dtype=jnp.int32)[:, None]).astype(BF16)
    u = _nsa_prompt_attn(q_bf.reshape(B, T, -1), kvc, rows_bf.reshape(B, T, -1), gates_g,
                         zs.reshape(B, T, -1), slope_tab, expand)
    yp = _outproj(u.reshape(M, -1), w_out_bf, x2d, 512).reshape(B, T, D)
    out_p = (rows3[:, :, :2 * kvw].reshape(shp(B)), rows3[:, :, 2 * kvw:4 * kvw].reshape(shp(B)),
             rows3[:, T - min(WINDOW, T):, 4 * kvw:].reshape(shp(B)))

    cmp_pool, sel_pool, layer, page_table, win_buf = past
    B, T, D = xs.shape
    M = B * T
    x2d = xs.reshape(M, D)
    q, rows, _, zs, gates = _proj(x2d, norm_g, gains, zero_bias, w_main, w_small, _NSA_PLAN,
                                  _nsa_outs(F32), "sigmoid", M)
    kvc = _compress_paged(cmp_pool, layer, page_table, pe, w1, w2, k_gain[0])
    u = _nsa_sample_attn(q, kvc, sel_pool, layer, page_table, rows, win_buf, gates, zs, slope_tab, B, T)
    ys = _outproj(u, w_out_bf, x2d, M).reshape(B, T, D)
    rows3 = rows.reshape(B, T, -1)
    win_rows = rows3[:, :, 4 * kvw:].reshape(shp(B))
    out_s = (rows3[:, :, :2 * kvw].reshape(shp(B)), rows3[:, :, 2 * kvw:4 * kvw].reshape(shp(B)),
             jnp.concatenate([win_buf[:, T:], win_rows], axis=1))
    return yp, ys, out_p, out_s


_FOX_PLAN = (
    (0, 4, "normq", 0, (0,)),
    (4, 8, "norm", 1, (1, 2)),
    (8, 12, "raw", 0, (1, 2)),
    (12, 16, "silu", 0, (3,)),
)


def _fox_outs(act_dtype):
    return ((0, 4, act_dtype), (4, 8, F32), (4, 8, act_dtype), (12, 4, act_dtype))


def _fox_layer(xp, xs, norm_g, w_in, f_bias, q_gain, k_gain, w_out, past):
    mix = N_HEADS * HEAD_DIM
    w_main = jnp.concatenate([w_in[:, :3 * mix], w_in[:, 3 * mix + N_HEADS:]], axis=1).astype(BF16)
    w_small = jnp.pad(w_in[:, 3 * mix:3 * mix + N_HEADS], ((0, 0), (0, LANES - N_HEADS))).astype(BF16)
    gains = jnp.concatenate([q_gain[None], k_gain[None], jnp.zeros((6, HEAD_DIM), F32)], axis=0)
    sbias = jnp.pad(f_bias, (0, LANES - N_HEADS)).reshape(1, LANES)
    w_out_bf = w_out.astype(BF16)

    B, T, D = xp.shape
    M = B * T
    x2d = xp.reshape(M, D)
    q_bf, kv, kv_bf, zs, logf = _proj(x2d, norm_g, gains, sbias, w_main, w_small, _FOX_PLAN, _fox_outs(BF16),
                                      "logsig", 512)
    cum = _cumsum_heads(logf.reshape(B, T, LANES))[:, :N_HEADS].reshape(B, N_HEADS, 1, T)
    u = _fox_prompt_attn(q_bf.reshape(B, T, -1), kv_bf.reshape(B, T, -1), cum, zs.reshape(B, T, -1))
    yp = _outproj(u.reshape(M, -1), w_out_bf, x2d, 512).reshape(B, T, D)
    out_p = (kv.reshape(B, T, 2, N_HEADS, HEAD_DIM), logf[:, :N_HEADS].reshape(B, T, N_HEADS))

    kv_pool, logf_pool, layer, page_table = past
    B, T, D = xs.shape
    M = B * T
    x2d = xs.reshape(M, D)
    q, kv, _, zs, logf = _proj(x2d, norm_g, gains, sbias, w_main, w_small, _FOX_PLAN, _fox_outs(F32),
                               "logsig", M)
    cum = _cumsum_paged(logf_pool, layer, page_table)
    u = _fox_sample_attn(q, kv_pool, layer, page_table, cum, kv, logf, zs, B, T)
    ys = _outproj(u, w_out_bf, x2d, M).reshape(B, T, D)
    out_s = (kv.reshape(B, T, 2, N_HEADS, HEAD_DIM), logf[:, :N_HEADS].reshape(B, T, N_HEADS))
    return yp, ys, out_p, out_s


def kernel(x_prompt, x_sample, cache_a_cmp, cache_a_sel, state_a_win, cache_b_kv, cache_b_logf, page_table,
           a_norm, a_w_in, a_q_gain, a_k_gain, a_phi_pe, a_phi_w1, a_phi_w2, a_w_out,
           b_norm, b_w_in, b_f_bias, b_q_gain, b_k_gain, b_w_out):
    slopes = _alibi_slopes()
    xp, xs = x_prompt, x_sample
    depth = a_norm.shape[0] + b_norm.shape[0]
    a_p, a_s, b_p, b_s = [], [], [], []
    for i in range(depth):
        j = i // 2
        if i % 2 == 0:
            xp, xs, out_p, out_s = _nsa_layer(
                xp, xs, a_norm[j], a_w_in[j], a_q_gain[j], a_k_gain[j], a_phi_pe[j], a_phi_w1[j], a_phi_w2[j],
                a_w_out[j], slopes, (cache_a_cmp, cache_a_sel, j, page_table, state_a_win[j]))
            a_p.append(out_p)
            a_s.append(out_s)
        else:
            xp, xs, out_p, out_s = _fox_layer(
                xp, xs, b_norm[j], b_w_in[j], b_f_bias[j], b_q_gain[j], b_k_gain[j], b_w_out[j],
                (cache_b_kv, cache_b_logf, j, page_table))
            b_p.append(out_p)
            b_s.append(out_s)
    stack = lambda outs, k: jnp.stack([o[k] for o in outs])
    return (xp, xs,
            stack(a_p, 0), stack(a_s, 0), stack(a_p, 1), stack(a_s, 1), stack(a_p, 2), stack(a_s, 2),
            stack(b_p, 0), stack(b_s, 0), stack(b_p, 1), stack(b_s, 1))
```

```python
import functools

import jax
import jax.numpy as jnp
import numpy as np
from jax import lax
from jax.experimental import pallas as pl
from jax.experimental.pallas import tpu as pltpu

HEAD_DIM = 128
N_HEADS = 16
A_KV_GROUPS = 4
A_REP = N_HEADS // A_KV_GROUPS
CMP_BLOCK = 64
SEL_BLOCK = CMP_BLOCK
TOP_N = 16
WINDOW = 512
RMS_EPS = 1e-6
SCALE = HEAD_DIM ** -0.5
FORCED_SCORE = float(A_REP + 1)
NEG_BIG = -1e30

LANES = 128
SUBLANES = 8
PROJ_COLS = 512
VMEM_LIMIT = 56 * 1024 * 1024
CMP_PAGES = 16
SEL_PAGES = 8
FOX_PAGES = 4
LOGF_PAGES = 8

F32 = jnp.float32
BF16 = jnp.bfloat16
NT = (((1,), (1,)), ((), ()))
TN = (((0,), (0,)), ((), ()))


def _cparams(sem):
    return pltpu.CompilerParams(dimension_semantics=sem, vmem_limit_bytes=VMEM_LIMIT)


def _head_rms(a, gain):
    ms = jnp.mean(a * a, axis=-1, keepdims=True)
    return a * lax.rsqrt(ms + RMS_EPS) * gain


def _softmax_rows(s, ok):
    m = jnp.max(s, axis=-1, keepdims=True)
    p = jnp.where(ok, jnp.exp(s - m), 0.0)
    return p / jnp.maximum(jnp.sum(p, axis=-1, keepdims=True), 1e-30)


def _split3(x):
    hi = x.astype(BF16)
    r1 = x - hi.astype(F32)
    mid = r1.astype(BF16)
    lo = (r1 - mid.astype(F32)).astype(BF16)
    return hi, mid, lo


def _lower_ones(n):
    tri = lax.broadcasted_iota(jnp.int32, (n, n), 0) >= lax.broadcasted_iota(jnp.int32, (n, n), 1)
    return jnp.where(tri, 1.0, 0.0).astype(BF16)


def _tile_row(block_ref, j):
    n, _, d = block_ref.shape
    return block_ref.reshape(n * SUBLANES, d)[pl.ds(j, n, stride=SUBLANES), :]


def _rank_counts(score, blk, n_blk):
    rank = jnp.zeros(score.shape, F32)
    for i in range(n_blk):
        col = score[:, i:i + 1]
        rank = rank + jnp.where(blk > i, jnp.where(col >= score, 1.0, 0.0), jnp.where(col > score, 1.0, 0.0))
    return rank


def _proj_kernel(plan, small_kind, x_ref, ng_ref, gains_ref, sbias_ref, w_ref, ws_ref, *refs):
    n_out = len(refs) - 1
    outs, xn_ref = refs[:n_out], refs[n_out]
    j = pl.program_id(1)

    @pl.when(j == 0)
    def _():
        x = x_ref[...]
        ms = jnp.mean(x * x, axis=-1, keepdims=True)
        xn_ref[...] = (x * lax.rsqrt(ms + RMS_EPS) * ng_ref[...]).astype(BF16)
        small = jnp.dot(xn_ref[...], ws_ref[...], preferred_element_type=F32)
        if small_kind == "sigmoid":
            small = jax.nn.sigmoid(small)
        else:
            small = small + sbias_ref[...]
            small = jnp.minimum(small, 0.0) - jnp.log1p(jnp.exp(-jnp.abs(small)))
        outs[n_out - 1][...] = small

    for (j0, j1, kind, gain_row, targets) in plan:
        @pl.when((j >= j0) & (j < j1))
        def _(kind=kind, gain_row=gain_row, targets=targets):
            acc = jnp.dot(xn_ref[...], w_ref[...], preferred_element_type=F32)
            if kind in ("norm", "normq"):
                g = gains_ref[gain_row:gain_row + 1, :]
                if kind == "normq":
                    g = g * SCALE
                acc = jnp.concatenate(
                    [_head_rms(acc[:, h * LANES:(h + 1) * LANES], g) for h in range(PROJ_COLS // LANES)], axis=1)
            elif kind == "silu":
                acc = acc * jax.nn.sigmoid(acc)
            for t in targets:
                outs[t][...] = acc.astype(outs[t].dtype)


def _proj(x2d, norm_gain, gains, sbias, w_main, w_small, plan, out_defs, small_kind, bm):
    M, D = x2d.shape
    nblk = w_main.shape[1] // PROJ_COLS
    grid = (M // bm, nblk)

    def out_map(j0, n):
        return lambda i, j: (i, jnp.clip(j - j0, 0, n - 1))

    out_shapes = [jax.ShapeDtypeStruct((M, n * PROJ_COLS), dt) for (_, n, dt) in out_defs]
    out_specs = [pl.BlockSpec((bm, PROJ_COLS), out_map(j0, n)) for (j0, n, _) in out_defs]
    out_shapes.append(jax.ShapeDtypeStruct((M, LANES), F32))
    out_specs.append(pl.BlockSpec((bm, LANES), lambda i, j: (i, 0)))
    return pl.pallas_call(
        functools.partial(_proj_kernel, plan, small_kind),
        grid=grid,
        in_specs=[
            pl.BlockSpec((bm, D), lambda i, j: (i, 0)),
            pl.BlockSpec((1, D), lambda i, j: (0, 0)),
            pl.BlockSpec((8, LANES), lambda i, j: (0, 0)),
            pl.BlockSpec((1, LANES), lambda i, j: (0, 0)),
            pl.BlockSpec((D, PROJ_COLS), lambda i, j: (0, j)),
            pl.BlockSpec((D, LANES), lambda i, j: (0, 0)),
        ],
        out_specs=out_specs,
        out_shape=out_shapes,
        scratch_shapes=[pltpu.VMEM((bm, D), BF16)],
        compiler_params=_cparams(("parallel", "arbitrary")),
        name="proj",
    )(x2d, norm_gain.reshape(1, D), gains, sbias, w_main, w_small)


def _outproj_kernel(u_ref, w_ref, x_ref, o_ref):
    o_ref[...] = x_ref[...] + jnp.dot(u_ref[...].astype(BF16), w_ref[...], preferred_element_type=F32)


def _outproj(u2d, w_bf, x2d, bm):
    M, K = u2d.shape
    N = w_bf.shape[1]
    return pl.pallas_call(
        _outproj_kernel,
        grid=(M // bm,),
        in_specs=[
            pl.BlockSpec((bm, K), lambda i: (i, 0)),
            pl.BlockSpec((K, N), lambda i: (0, 0)),
            pl.BlockSpec((bm, N), lambda i: (i, 0)),
        ],
        out_specs=pl.BlockSpec((bm, N), lambda i: (i, 0)),
        out_shape=jax.ShapeDtypeStruct((M, N), F32),
        compiler_params=_cparams(("parallel",)),
        name="outproj",
    )(u2d, w_bf, x2d)


def _compress_kernel(n_blk, x0_ref, x1_ref, x2_ref, x3_ref, pe_ref, w1_ref, w2_ref, kg_ref, o_ref):
    kv = pl.program_id(1)

    def body(i, acc):
        pe = pe_ref[pl.ds(i, 1), :]
        a = jnp.concatenate(
            [x_ref[pl.ds(i, n_blk, stride=CMP_BLOCK), :] + pe for x_ref in (x0_ref, x1_ref, x2_ref, x3_ref)],
            axis=0)
        return acc + jnp.dot(a.astype(BF16), w1_ref[i], preferred_element_type=F32)

    acc = lax.fori_loop(0, CMP_BLOCK, body, jnp.zeros((A_KV_GROUPS * n_blk, HEAD_DIM), F32))
    h = jax.nn.gelu(acc)
    out = jnp.dot(h.astype(BF16), w2_ref[...], preferred_element_type=F32)
    out = jnp.where(kv == 0, _head_rms(out, kg_ref[...]), out)
    for g in range(A_KV_GROUPS):
        o_ref[g] = out[g * n_blk:(g + 1) * n_blk].astype(o_ref.dtype)


def _compress_prompt(rows, pe, w1_bf, w2_bf, k_gain_cmp):
    B, T = rows.shape[0], rows.shape[1]
    n_blk = T // CMP_BLOCK
    G = A_KV_GROUPS
    grp = lambda g: pl.BlockSpec((None, T, HEAD_DIM), lambda b, kv, g=g: (b, 0, kv * G + g))
    return pl.pallas_call(
        functools.partial(_compress_kernel, n_blk),
        grid=(B, 2),
        in_specs=[
            grp(0), grp(1), grp(2), grp(3),
            pl.BlockSpec((None, CMP_BLOCK, HEAD_DIM), lambda b, kv: (kv, 0, 0)),
            pl.BlockSpec((None, CMP_BLOCK, HEAD_DIM, HEAD_DIM), lambda b, kv: (kv, 0, 0, 0)),
            pl.BlockSpec((None, HEAD_DIM, HEAD_DIM), lambda b, kv: (kv, 0, 0)),
            pl.BlockSpec((1, HEAD_DIM), lambda b, kv: (0, 0)),
        ],
        out_specs=pl.BlockSpec((None, None, A_KV_GROUPS, n_blk, HEAD_DIM), lambda b, kv: (b, kv, 0, 0, 0)),
        out_shape=jax.ShapeDtypeStruct((B, 2, A_KV_GROUPS, n_blk, HEAD_DIM), BF16),
        compiler_params=_cparams(("parallel", "arbitrary")),
        name="compress",
    )(rows, rows, rows, rows, pe, w1_bf, w2_bf, k_gain_cmp.reshape(1, HEAD_DIM))


def _compress_paged_kernel(n_pg, bpp, pt_ref, *refs):
    pages = refs[:n_pg]
    pe_ref, w1_ref, w2_ref, kg_ref, o_ref = refs[n_pg:]
    n_rows = n_pg * bpp * SUBLANES
    slabs = []
    for i in range(CMP_BLOCK):
        pe = pe_ref[i]
        a = jnp.concatenate([pg[h * CMP_BLOCK + i] + pe for pg in pages for h in range(bpp)], axis=0)
        slabs.append(a.astype(BF16))
    acc = jnp.dot(jnp.concatenate(slabs, axis=1), w1_ref[...], preferred_element_type=F32)
    h = jax.nn.gelu(acc)
    out_k = jnp.dot(h[:, :HEAD_DIM].astype(BF16), w2_ref[0], preferred_element_type=F32)
    out_v = jnp.dot(h[:, HEAD_DIM:].astype(BF16), w2_ref[1], preferred_element_type=F32)
    out_k = _head_rms(out_k, kg_ref[...])
    is_k = (lax.broadcasted_iota(jnp.int32, (n_rows, 1), 0) & (SUBLANES - 1)) < A_KV_GROUPS
    o_ref[...] = jnp.where(is_k, out_k, out_v)


def _compress_paged(pool, layer, page_table, pe, w1, w2, k_gain_cmp):
    page = pool.shape[2]
    bpp = page // CMP_BLOCK
    B, n_pages = page_table.shape
    n_pg = CMP_PAGES
    assert page % CMP_BLOCK == 0 and n_pages % n_pg == 0 and 2 * A_KV_GROUPS == SUBLANES
    pool5 = pool.reshape(pool.shape[0], pool.shape[1], page, SUBLANES, HEAD_DIM)
    pe8 = jnp.repeat(jnp.transpose(pe, (1, 0, 2)), A_KV_GROUPS, axis=1)
    w1cat = jnp.concatenate([w1[0], w1[1]], axis=-1).astype(BF16)
    w1cat = w1cat.reshape(CMP_BLOCK * HEAD_DIM, 2 * HEAD_DIM)
    rows_per_step = n_pg * bpp * SUBLANES
    page_spec = lambda p: pl.BlockSpec(
        (None, None, page, SUBLANES, HEAD_DIM), lambda b, c, pt, p=p: (layer, pt[b, c * n_pg + p], 0, 0, 0))
    return pl.pallas_call(
        functools.partial(_compress_paged_kernel, n_pg, bpp),
        grid_spec=pltpu.PrefetchScalarGridSpec(
            num_scalar_prefetch=1,
            grid=(B, n_pages // n_pg),
            in_specs=[page_spec(p) for p in range(n_pg)] + [
                pl.BlockSpec((CMP_BLOCK, SUBLANES, HEAD_DIM), lambda b, c, pt: (0, 0, 0)),
                pl.BlockSpec((CMP_BLOCK * HEAD_DIM, 2 * HEAD_DIM), lambda b, c, pt: (0, 0)),
                pl.BlockSpec((2, HEAD_DIM, HEAD_DIM), lambda b, c, pt: (0, 0, 0)),
                pl.BlockSpec((1, HEAD_DIM), lambda b, c, pt: (0, 0)),
            ],
            out_specs=pl.BlockSpec((None, rows_per_step, HEAD_DIM), lambda b, c, pt: (b, c, 0)),
        ),
        out_shape=jax.ShapeDtypeStruct((B, n_pages * bpp * SUBLANES, HEAD_DIM), F32),
        compiler_params=_cparams(("parallel", "arbitrary")),
        name="compress_paged",
    )(page_table, *([pool5] * n_pg), pe8, w1cat, w2.astype(BF16), k_gain_cmp.reshape(1, HEAD_DIM))


def _select_blocks_t(imp_t, cur, n_blk):
    blk = lax.broadcasted_iota(jnp.int32, (n_blk, 1), 0)
    valid = blk <= cur
    forced = (blk == 0) | (blk == cur) | (blk == cur - 1)
    score = jnp.where(valid, jnp.where(forced, FORCED_SCORE, imp_t), -jnp.inf)
    sub = lax.broadcasted_iota(jnp.int32, (SUBLANES, 1), 0)
    ranks = []
    for b0 in range(0, n_blk, SUBLANES):
        slab = score[b0:b0 + SUBLANES]
        rank = jnp.zeros(slab.shape, F32)
        for i in range(n_blk):
            row = score[i:i + 1]
            if i < b0:
                rank = rank + jnp.where(row >= slab, 1.0, 0.0)
            elif i >= b0 + SUBLANES:
                rank = rank + jnp.where(row > slab, 1.0, 0.0)
            else:
                rank = rank + jnp.where(sub > i - b0, jnp.where(row >= slab, 1.0, 0.0),
                                        jnp.where(row > slab, 1.0, 0.0))
        ranks.append(rank)
    rank = jnp.concatenate(ranks, axis=0)
    return jnp.where(valid & (rank < float(min(TOP_N, n_blk))), 1.0, 0.0)


def _nsa_prompt_kernel(tq, tk, T, q_ref, kc_ref, vc_ref, ks_ref, vs_ref, kw_ref, vw_ref, gt_ref, zs_ref,
                       slope_ref, exp_ref, o_ref):
    qi = pl.program_id(2)
    t0 = qi * tq
    n_cmp = T // CMP_BLOCK
    R = A_REP
    q = q_ref[...]
    q4 = jnp.concatenate([q[:, r * HEAD_DIM:(r + 1) * HEAD_DIM] for r in range(R)], axis=0)
    qpos = t0 + lax.broadcasted_iota(jnp.int32, (tq, 1), 0)
    qpos_row = t0 + lax.broadcasted_iota(jnp.int32, (1, tq), 1)
    slopes = [slope_ref[r:r + 1, 0:1] for r in range(R)]

    sct = lax.dot_general(kc_ref[...], q4, NT, preferred_element_type=F32)
    c_end = (lax.broadcasted_iota(jnp.int32, (n_cmp, 1), 0) + 1) * CMP_BLOCK - 1
    ok_c = qpos_row >= c_end
    cposf = (c_end - t0).astype(F32)
    p_parts = []
    for r in range(R):
        s_r = jnp.where(ok_c, sct[:, r * tq:(r + 1) * tq] + slopes[r] * cposf, NEG_BIG)
        p_r = jnp.where(ok_c, jnp.exp(s_r - jnp.max(s_r, axis=0, keepdims=True)), 0.0)
        p_parts.append(p_r / jnp.maximum(jnp.sum(p_r, axis=0, keepdims=True), 1e-30))
    p_ct = jnp.concatenate(p_parts, axis=1).astype(BF16)
    o_cmp = lax.dot_general(p_ct, vc_ref[...], TN, preferred_element_type=F32)
    imp_t = p_parts[0]
    for r in range(1, R):
        imp_t = imp_t + p_parts[r]
    sel_t = _select_blocks_t(imp_t, qpos_row // SEL_BLOCK, n_cmp).astype(BF16)

    def scores(kt):
        k0 = pl.multiple_of(kt * tk, tk)
        return lax.dot_general(q4, ks_ref[pl.ds(k0, tk), :], NT, preferred_element_type=F32)

    def sel_tile(kt, s, carry, diagonal):
        m, l, acc = carry
        k0 = pl.multiple_of(kt * tk, tk)
        v = vs_ref[pl.ds(k0, tk), :]
        kpos = k0 + lax.broadcasted_iota(jnp.int32, (1, tk), 1)
        picked = lax.dot_general(sel_t, exp_ref[:, pl.ds(k0, tk)], TN, preferred_element_type=F32)
        mask = (picked - 1.0) * (-NEG_BIG)
        if diagonal:
            mask = jnp.where(kpos <= qpos, mask, NEG_BIG)
        kposf = (kpos - t0).astype(F32)
        s = jnp.concatenate(
            [s[r * tq:(r + 1) * tq] + (mask + slopes[r] * kposf) for r in range(R)], axis=0)
        m_new = jnp.maximum(m, jnp.max(s, axis=-1, keepdims=True))
        alpha = jnp.exp(m - m_new)
        p = jnp.exp(s - m_new)
        l = alpha * l + jnp.sum(p, axis=-1, keepdims=True)
        acc = alpha * acc + jnp.dot(p.astype(BF16), v, preferred_element_type=F32)
        return m_new, l, acc

    n_full = t0 // tk
    init = (jnp.full((R * tq, 1), NEG_BIG, F32), jnp.zeros((R * tq, 1), F32), jnp.zeros((R * tq, HEAD_DIM), F32))
    carry = lax.fori_loop(0, n_full, lambda kt, c: sel_tile(kt, scores(kt), c, False), init)
    _, l_s, acc_s = sel_tile(n_full, scores(n_full), carry, True)
    o_sel = acc_s / l_s

    span = WINDOW + tq
    w0 = pl.multiple_of(jnp.maximum(t0 - WINDOW, 0), tq)
    kw = kw_ref[pl.ds(w0, span), :]
    vw = vw_ref[pl.ds(w0, span), :]
    sw = lax.dot_general(q4, kw, NT, preferred_element_type=F32)
    wpos = w0 + lax.broadcasted_iota(jnp.int32, (1, span), 1)
    d_w = qpos - wpos
    mask_w = jnp.where((d_w >= 0) & (d_w < WINDOW), 0.0, NEG_BIG)
    wposf = (wpos - t0).astype(F32)
    sw = jnp.concatenate(
        [sw[r * tq:(r + 1) * tq] + (mask_w + slopes[r] * wposf) for r in range(R)], axis=0)
    pw = jnp.exp(sw - jnp.max(sw, axis=-1, keepdims=True))
    o_win = (jnp.dot(pw.astype(BF16), vw, preferred_element_type=F32)
             / jnp.sum(pw, axis=-1, keepdims=True))

    gt = gt_ref[...]
    zs = zs_ref[...]
    outs = []
    for r in range(R):
        rows = slice(r * tq, (r + 1) * tq)
        o_r = (gt[:, r:r + 1] * o_cmp[rows] + gt[:, R + r:R + r + 1] * o_sel[rows]
               + gt[:, 2 * R + r:2 * R + r + 1] * o_win[rows])
        outs.append(o_r * zs[:, r * HEAD_DIM:(r + 1) * HEAD_DIM].astype(F32))
    o_ref[...] = jnp.concatenate(outs, axis=1).astype(o_ref.dtype)


def _nsa_prompt_attn(q_bf, kvc, rows_bf, gates_g, zs, slope_tab, expand, tq=128, tk=512):
    B, T, W = q_bf.shape
    G = A_KV_GROUPS
    gw = A_REP * HEAD_DIM
    n_cmp = T // CMP_BLOCK
    tk = min(tk, T)
    assert T % tq == 0 and T % tk == 0 and T >= WINDOW + tq
    full = lambda col: pl.BlockSpec((None, T, HEAD_DIM), lambda b, g, i, col=col: (b, 0, col * G + g))
    return pl.pallas_call(
        functools.partial(_nsa_prompt_kernel, tq, tk, T),
        grid=(B, G, T // tq),
        in_specs=[
            pl.BlockSpec((None, tq, gw), lambda b, g, i: (b, i, g)),
            pl.BlockSpec((None, None, None, n_cmp, HEAD_DIM), lambda b, g, i: (b, 0, g, 0, 0)),
            pl.BlockSpec((None, None, None, n_cmp, HEAD_DIM), lambda b, g, i: (b, 1, g, 0, 0)),
            full(0), full(1), full(2), full(3),
            pl.BlockSpec((None, None, tq, 3 * A_REP), lambda b, g, i: (b, g, i, 0)),
            pl.BlockSpec((None, tq, gw), lambda b, g, i: (b, i, g)),
            pl.BlockSpec((None, 8, LANES), lambda b, g, i: (g, 0, 0)),
            pl.BlockSpec((n_cmp, T), lambda b, g, i: (0, 0)),
        ],
        out_specs=pl.BlockSpec((None, tq, gw), lambda b, g, i: (b, i, g)),
        out_shape=jax.ShapeDtypeStruct((B, T, W), BF16),
        compiler_params=_cparams(("parallel", "parallel", "arbitrary")),
        name="nsa_prompt_attn",
    )(q_bf, kvc, kvc, rows_bf, rows_bf, rows_bf, rows_bf, gates_g, zs, slope_tab, expand)


def _nsa_sample_kernel(n_pg, page, P_len, T, wb, pt_ref, q_ref, kvc_ref, *refs):
    pages = refs[:n_pg]
    rows_ref, win_ref, gt_ref, zs_ref, slope_ref, o_ref, m_ref, l_ref, acc_ref, sel_ref, ocmp_ref = refs[n_pg:]
    c = pl.program_id(1)
    G, R = A_KV_GROUPS, A_REP
    n_blk = P_len // SEL_BLOCK
    q = q_ref[...]
    tpos = lax.broadcasted_iota(jnp.int32, (T, 1), 0)

    def q4(g):
        return jnp.concatenate(
            [q[:, (g * R + r) * HEAD_DIM:(g * R + r + 1) * HEAD_DIM] for r in range(R)], axis=0).astype(BF16)

    def slope(g, r):
        return slope_ref[g, r:r + 1, 0:1]

    def flash_update(g, k, v, ok, kposf):
        s = lax.dot_general(q4(g), k, NT, preferred_element_type=F32)
        s = jnp.concatenate(
            [jnp.where(ok, s[r * T:(r + 1) * T] + slope(g, r) * kposf, NEG_BIG) for r in range(R)], axis=0)
        ok4 = jnp.concatenate([ok] * R, axis=0)
        m = m_ref[g][:, 0:1]
        m_new = jnp.maximum(m, jnp.max(s, axis=-1, keepdims=True))
        alpha = jnp.exp(m - m_new)
        p = jnp.where(ok4, jnp.exp(s - m_new), 0.0)
        l_new = alpha * l_ref[g][:, 0:1] + jnp.sum(p, axis=-1, keepdims=True)
        acc_ref[g] = alpha * acc_ref[g] + jnp.dot(p.astype(BF16), v, preferred_element_type=F32)
        m_ref[g] = jnp.broadcast_to(m_new, (R * T, LANES))
        l_ref[g] = jnp.broadcast_to(l_new, (R * T, LANES))

    @pl.when(c == 0)
    def _():
        c_end = (lax.broadcasted_iota(jnp.int32, (1, n_blk), 1) + 1) * CMP_BLOCK - 1
        cposf = (c_end - P_len).astype(F32)
        ok_c = (P_len + tpos) >= c_end
        imps = []
        for g in range(G):
            kc = kvc_ref[pl.ds(g, n_blk, stride=SUBLANES), :].astype(BF16)
            vc = kvc_ref[pl.ds(G + g, n_blk, stride=SUBLANES), :].astype(BF16)
            sc = lax.dot_general(q4(g), kc, NT, preferred_element_type=F32)
            parts = []
            for r in range(R):
                s_r = sc[r * T:(r + 1) * T] + slope(g, r) * cposf
                parts.append(_softmax_rows(jnp.where(ok_c, s_r, NEG_BIG), ok_c))
            ocmp_ref[g] = jnp.dot(jnp.concatenate(parts, axis=0).astype(BF16), vc, preferred_element_type=F32)
            imps.append(parts[0] + parts[1] + parts[2] + parts[3])
        imp = jnp.concatenate(imps, axis=0)
        blk = lax.broadcasted_iota(jnp.int32, (1, n_blk), 1)
        forced = (blk == 0) | (blk == n_blk - 1)
        score = jnp.where(forced, FORCED_SCORE, imp)
        rank = _rank_counts(score, blk, n_blk) + jnp.where(FORCED_SCORE > score, 1.0, 0.0)
        sel_ref[...] = jnp.where(rank < float(TOP_N), 1.0, 0.0)
        m_ref[...] = jnp.full(m_ref.shape, NEG_BIG, F32)
        l_ref[...] = jnp.zeros(l_ref.shape, F32)
        acc_ref[...] = jnp.zeros(acc_ref.shape, F32)

    n_keys = n_pg * page
    kpos = c * n_keys + lax.broadcasted_iota(jnp.int32, (1, n_keys), 1)
    kposf = (kpos - P_len).astype(F32)
    rowblk = lax.broadcasted_iota(jnp.int32, (n_blk, 1), 0)
    expand = jnp.where(rowblk == kpos // SEL_BLOCK, 1.0, 0.0).astype(BF16)
    picked = jnp.dot(sel_ref[...].astype(BF16), expand, preferred_element_type=F32)
    for g in range(G):
        k = jnp.concatenate([pg[pl.ds(g, page, stride=SUBLANES), :] for pg in pages], axis=0).astype(BF16)
        v = jnp.concatenate([pg[pl.ds(G + g, page, stride=SUBLANES), :] for pg in pages], axis=0).astype(BF16)
        flash_update(g, k, v, picked[g * T:(g + 1) * T] > 0.5, kposf)

    @pl.when(c == pl.num_programs(1) - 1)
    def _():
        kvw = G * HEAD_DIM
        rows = rows_ref[...]
        gt = gt_ref[...]
        zs = zs_ref[...]
        pad = jnp.zeros((LANES - T, HEAD_DIM), F32)
        lane = lax.broadcasted_iota(jnp.int32, (1, LANES), 1)
        widx = lax.broadcasted_iota(jnp.int32, (1, wb + LANES), 1)
        ok_w = (widx > tpos + (wb - WINDOW)) & (widx <= tpos + wb)
        wposf = (widx - wb).astype(F32)
        outs = []
        for g in range(G):
            new = lambda sec: jnp.concatenate(
                [rows[:, sec * kvw + g * HEAD_DIM:sec * kvw + (g + 1) * HEAD_DIM], pad], axis=0)
            flash_update(g, new(2).astype(BF16), new(3).astype(BF16), lane <= tpos, lane.astype(F32))
            o_sel = acc_ref[g] / jnp.maximum(l_ref[g][:, 0:1], 1e-30)
            kw = jnp.concatenate([win_ref[pl.ds(g, wb, stride=SUBLANES), :], new(4)], axis=0).astype(BF16)
            vw = jnp.concatenate([win_ref[pl.ds(G + g, wb, stride=SUBLANES), :], new(5)], axis=0).astype(BF16)
            sw = lax.dot_general(q4(g), kw, NT, preferred_element_type=F32)
            parts = []
            for r in range(R):
                s_r = sw[r * T:(r + 1) * T] + slope(g, r) * wposf
                parts.append(_softmax_rows(jnp.where(ok_w, s_r, NEG_BIG), ok_w))
            o_win = jnp.dot(jnp.concatenate(parts, axis=0).astype(BF16), vw, preferred_element_type=F32)
            o_cmp = ocmp_ref[g]
            for r in range(R):
                h = g * R + r
                rs = slice(r * T, (r + 1) * T)
                o_r = (gt[:, h:h + 1] * o_cmp[rs] + gt[:, N_HEADS + h:N_HEADS + h + 1] * o_sel[rs]
                       + gt[:, 2 * N_HEADS + h:2 * N_HEADS + h + 1] * o_win[rs])
                outs.append(o_r * zs[:, h * HEAD_DIM:(h + 1) * HEAD_DIM])
        o_ref[...] = jnp.concatenate(outs, axis=1)


def _nsa_sample_attn(q, kvc, sel_pool, layer, page_table, rows, win_buf, gates, zs, slope_tab, B, T):
    page = sel_pool.shape[2]
    n_pages = page_table.shape[1]
    P_len = n_pages * page
    wb = win_buf.shape[1]
    n_pg = SEL_PAGES
    n_blk = P_len // SEL_BLOCK
    G, R = A_KV_GROUPS, A_REP
    W = N_HEADS * HEAD_DIM
    assert P_len % SEL_BLOCK == 0 and T <= SEL_BLOCK and T % SUBLANES == 0 and n_blk + 1 > TOP_N
    assert n_pages % n_pg == 0 and WINDOW <= wb <= P_len and T <= LANES
    pool4 = sel_pool.reshape(sel_pool.shape[0], sel_pool.shape[1], page * SUBLANES, HEAD_DIM)
    win3 = win_buf.reshape(B, wb * SUBLANES, HEAD_DIM)
    page_spec = lambda p: pl.BlockSpec(
        (None, None, page * SUBLANES, HEAD_DIM), lambda b, c, pt, p=p: (layer, pt[b, c * n_pg + p], 0, 0))
    tok = lambda width: pl.BlockSpec((T, width), lambda b, c, pt: (b, 0))
    return pl.pallas_call(
        functools.partial(_nsa_sample_kernel, n_pg, page, P_len, T, wb),
        grid_spec=pltpu.PrefetchScalarGridSpec(
            num_scalar_prefetch=1,
            grid=(B, n_pages // n_pg),
            in_specs=[tok(W), pl.BlockSpec((None, n_blk * SUBLANES, HEAD_DIM), lambda b, c, pt: (b, 0, 0))]
            + [page_spec(p) for p in range(n_pg)] + [
                tok(rows.shape[1]),
                pl.BlockSpec((None, wb * SUBLANES, HEAD_DIM), lambda b, c, pt: (b, 0, 0)),
                tok(LANES), tok(W),
                pl.BlockSpec((G, 8, LANES), lambda b, c, pt: (0, 0, 0)),
            ],
            out_specs=tok(W),
            scratch_shapes=[
                pltpu.VMEM((G, R * T, LANES), F32), pltpu.VMEM((G, R * T, LANES), F32),
                pltpu.VMEM((G, R * T, HEAD_DIM), F32), pltpu.VMEM((G * T, n_blk), F32),
                pltpu.VMEM((G, R * T, HEAD_DIM), F32),
            ],
        ),
        out_shape=jax.ShapeDtypeStruct((B * T, W), F32),
        compiler_params=_cparams(("parallel", "arbitrary")),
        name="nsa_sample_attn",
    )(page_table, q, kvc, *([pool4] * n_pg), rows, win3, gates, zs, slope_tab)


def _cumsum_kernel(tb, lf_ref, o_ref, carry_ref):
    @pl.when(pl.program_id(1) == 0)
    def _():
        carry_ref[...] = jnp.zeros_like(carry_ref)

    lower = _lower_ones(tb)
    acc = jnp.zeros((tb, LANES), F32)
    for part in _split3(lf_ref[...]):
        acc = acc + jnp.dot(lower, part, preferred_element_type=F32)
    acc = acc.T + carry_ref[...]
    o_ref[...] = acc
    carry_ref[...] = acc[:, tb - 1:tb]


def _cumsum_heads(logf, tb=256):
    B, T, _ = logf.shape
    tb = min(tb, T)
    return pl.pallas_call(
        functools.partial(_cumsum_kernel, tb),
        grid=(B, T // tb),
        in_specs=[pl.BlockSpec((None, tb, LANES), lambda b, i: (b, i, 0))],
        out_specs=pl.BlockSpec((None, LANES, tb), lambda b, i: (b, 0, i)),
        out_shape=jax.ShapeDtypeStruct((B, LANES, T), F32),
        scratch_shapes=[pltpu.VMEM((LANES, 1), F32)],
        compiler_params=_cparams(("parallel", "arbitrary")),
        name="fox_cumsum",
    )(logf)


def _cumsum_paged_kernel(n_pg, page, H, pt_ref, *refs):
    pages = refs[:n_pg]
    o_ref, carry_ref = refs[n_pg:]

    @pl.when(pl.program_id(1) == 0)
    def _():
        carry_ref[...] = jnp.zeros_like(carry_ref)

    lower = _lower_ones(page)
    widen = (lax.broadcasted_iota(jnp.int32, (H, LANES), 0) == lax.broadcasted_iota(jnp.int32, (H, LANES), 1))
    widen = jnp.where(widen, 1.0, 0.0).astype(BF16)
    carry = carry_ref[...]
    for p, pg in enumerate(pages):
        acc = jnp.zeros((page, LANES), F32)
        for part in _split3(pg[...]):
            wide = jnp.dot(part, widen, preferred_element_type=F32).astype(BF16)
            acc = acc + jnp.dot(lower, wide, preferred_element_type=F32)
        cum = acc.T + carry
        o_ref[:, p * page:(p + 1) * page] = cum
        carry = cum[:, page - 1:page]
    carry_ref[...] = carry


def _cumsum_paged(pool, layer, page_table):
    page, H = pool.shape[2], pool.shape[3]
    B, n_pages = page_table.shape
    n_pg = LOGF_PAGES
    assert n_pages % n_pg == 0 and page % LANES == 0
    page_spec = lambda p: pl.BlockSpec(
        (None, None, page, H), lambda b, c, pt, p=p: (layer, pt[b, c * n_pg + p], 0, 0))
    return pl.pallas_call(
        functools.partial(_cumsum_paged_kernel, n_pg, page, H),
        grid_spec=pltpu.PrefetchScalarGridSpec(
            num_scalar_prefetch=1,
            grid=(B, n_pages // n_pg),
            in_specs=[page_spec(p) for p in range(n_pg)],
            out_specs=pl.BlockSpec((None, LANES, n_pg * page), lambda b, c, pt: (b, 0, c)),
            scratch_shapes=[pltpu.VMEM((LANES, 1), F32)],
        ),
        out_shape=jax.ShapeDtypeStruct((B, LANES, n_pages * page), F32),
        compiler_params=_cparams(("parallel", "arbitrary")),
        name="fox_cumsum_paged",
    )(page_table, *([pool] * n_pg))


def _fox_prompt_kernel(tq, tk, q_ref, k_ref, v_ref, cum_ref, zs_ref, o_ref):
    qi = pl.program_id(2)
    t0 = qi * tq
    q = q_ref[...]
    qpos = t0 + lax.broadcasted_iota(jnp.int32, (tq, 1), 0)

    def scores(kt):
        k0 = pl.multiple_of(kt * tk, tk)
        return lax.dot_general(q, k_ref[pl.ds(k0, tk), :], NT, preferred_element_type=F32)

    def step(kt, s, carry, masked):
        m, l, acc = carry
        k0 = pl.multiple_of(kt * tk, tk)
        v = v_ref[pl.ds(k0, tk), :]
        s = s - cum_ref[:, pl.ds(k0, tk)]
        if masked:
            ok = (k0 + lax.broadcasted_iota(jnp.int32, (1, tk), 1)) <= qpos
            s = jnp.where(ok, s, NEG_BIG)
        m_new = jnp.maximum(m, jnp.max(s, axis=-1, keepdims=True))
        alpha = jnp.exp(m - m_new)
        p = jnp.exp(s - m_new)
        if masked:
            p = jnp.where(ok, p, 0.0)
        l = alpha * l + jnp.sum(p, axis=-1, keepdims=True)
        acc = alpha * acc + jnp.dot(p.astype(BF16), v, preferred_element_type=F32)
        return m_new, l, acc

    init = (jnp.full((tq, 1), NEG_BIG, F32), jnp.zeros((tq, 1), F32), jnp.zeros((tq, HEAD_DIM), F32))
    n_full = t0 // tk
    carry = lax.fori_loop(0, n_full, lambda kt, c: step(kt, scores(kt), c, False), init)
    for d in range(tq // tk):
        carry = step(n_full + d, scores(n_full + d), carry, True)
    _, l, acc = carry
    o_ref[...] = (acc / jnp.maximum(l, 1e-30) * zs_ref[...].astype(F32)).astype(o_ref.dtype)


def _fox_prompt_attn(q_bf, kv_bf, cum, zs, tq=512, tk=512):
    B, T, W = q_bf.shape
    H = N_HEADS
    tq, tk = min(tq, T), min(tk, T)
    assert T % tq == 0 and tq % tk == 0
    return pl.pallas_call(
        functools.partial(_fox_prompt_kernel, tq, tk),
        grid=(B, H, T // tq),
        in_specs=[
            pl.BlockSpec((None, tq, HEAD_DIM), lambda b, h, i: (b, i, h)),
            pl.BlockSpec((None, T, HEAD_DIM), lambda b, h, i: (b, 0, h)),
            pl.BlockSpec((None, T, HEAD_DIM), lambda b, h, i: (b, 0, H + h)),
            pl.BlockSpec((None, None, 1, T), lambda b, h, i: (b, h, 0, 0)),
            pl.BlockSpec((None, tq, HEAD_DIM), lambda b, h, i: (b, i, h)),
        ],
        out_specs=pl.BlockSpec((None, tq, HEAD_DIM), lambda b, h, i: (b, i, h)),
        out_shape=jax.ShapeDtypeStruct((B, T, W), BF16),
        compiler_params=_cparams(("parallel", "parallel", "arbitrary")),
        name="fox_prompt_attn",
    )(q_bf, kv_bf, kv_bf, cum, zs)


def _fox_sample_kernel(n_pg, page, T, pt_ref, q_ref, *refs):
    blocks = refs[:4 * n_pg]
    cum_ref, kvn_ref, lfn_ref, zs_ref, o_ref, m_ref, l_ref, acc_ref = refs[4 * n_pg:]
    c = pl.program_id(1)
    H = N_HEADS
    n_half = H // SUBLANES

    def head_rows(kv, h):
        return jnp.concatenate(
            [_tile_row(blocks[(p * 2 + kv) * n_half + h // SUBLANES], h % SUBLANES) for p in range(n_pg)], axis=0)

    q = q_ref[...].astype(BF16)

    @pl.when(c == 0)
    def _():
        m_ref[...] = jnp.full(m_ref.shape, NEG_BIG, F32)
        l_ref[...] = jnp.zeros(l_ref.shape, F32)
        acc_ref[...] = jnp.zeros(acc_ref.shape, F32)

    def attend(ks, vs, cum_rows, ok):
        n = ks[0].shape[0]
        s = jnp.concatenate(
            [lax.dot_general(q[:, h * HEAD_DIM:(h + 1) * HEAD_DIM], ks[h], NT, preferred_element_type=F32)
             - jnp.broadcast_to(cum_rows[h], (T, n)) for h in range(H)], axis=0)
        if ok is not None:
            ok = jnp.concatenate([ok] * H, axis=0)
            s = jnp.where(ok, s, NEG_BIG)
        m_new = jnp.maximum(m_ref[:, 0:1], jnp.max(s, axis=-1, keepdims=True))
        alpha = jnp.exp(m_ref[:, 0:1] - m_new)
        p = jnp.exp(s - m_new)
        if ok is not None:
            p = jnp.where(ok, p, 0.0)
        l_new = alpha * l_ref[:, 0:1] + jnp.sum(p, axis=-1, keepdims=True)
        pv = jnp.concatenate(
            [jnp.dot(p[h * T:(h + 1) * T].astype(BF16), vs[h], preferred_element_type=F32) for h in range(H)], axis=0)
        acc_ref[...] = alpha * acc_ref[...] + pv
        m_ref[...] = jnp.broadcast_to(m_new, (H * T, LANES))
        l_ref[...] = jnp.broadcast_to(l_new, (H * T, LANES))

    attend([head_rows(0, h).astype(BF16) for h in range(H)], [head_rows(1, h).astype(BF16) for h in range(H)],
           [cum_ref[h:h + 1, :] for h in range(H)], None)

    @pl.when(c == pl.num_programs(1) - 1)
    def _():
        total = cum_ref[:, n_pg * page - 1:n_pg * page]
        pad = jnp.zeros((LANES - T, LANES), F32)
        acc = jnp.zeros((LANES, LANES), F32)
        lower = _lower_ones(LANES)
        for part in _split3(jnp.concatenate([lfn_ref[...], pad], axis=0)):
            acc = acc + jnp.dot(lower, part, preferred_element_type=F32)
        cum_new = acc.T[:H] + total
        kvn = kvn_ref[...]
        zs = zs_ref[...]
        ok_n = lax.broadcasted_iota(jnp.int32, (1, LANES), 1) <= lax.broadcasted_iota(jnp.int32, (T, 1), 0)
        new = lambda j: jnp.concatenate([kvn[:, j * HEAD_DIM:(j + 1) * HEAD_DIM], pad], axis=0).astype(BF16)
        attend([new(h) for h in range(H)], [new(H + h) for h in range(H)],
               [cum_new[h:h + 1, :] for h in range(H)], ok_n)
        o = acc_ref[...] / jnp.maximum(l_ref[:, 0:1], 1e-30)
        o_ref[...] = jnp.concatenate(
            [o[h * T:(h + 1) * T] * zs[:, h * HEAD_DIM:(h + 1) * HEAD_DIM] for h in range(H)], axis=1)


def _fox_sample_attn(q, kv_pool, layer, page_table, cum, kv_new, logf_new, zs, B, T):
    page = kv_pool.shape[2]
    n_pages = page_table.shape[1]
    n_pg = FOX_PAGES
    H = N_HEADS
    W = H * HEAD_DIM
    assert n_pages % n_pg == 0 and T % SUBLANES == 0 and T <= LANES and H % SUBLANES == 0
    n_half = H // SUBLANES
    block_spec = lambda p, kv, half: pl.BlockSpec(
        (None, None, page, None, SUBLANES, HEAD_DIM),
        lambda b, c, pt: (layer, pt[b, c * n_pg + p], 0, kv, half, 0))
    block_specs = [block_spec(p, kv, half) for p in range(n_pg) for kv in range(2) for half in range(n_half)]
    tok = lambda width: pl.BlockSpec((T, width), lambda b, c, pt: (b, 0))
    return pl.pallas_call(
        functools.partial(_fox_sample_kernel, n_pg, page, T),
        grid_spec=pltpu.PrefetchScalarGridSpec(
            num_scalar_prefetch=1,
            grid=(B, n_pages // n_pg),
            in_specs=[tok(W)] + block_specs + [
                pl.BlockSpec((None, H, n_pg * page), lambda b, c, pt: (b, 0, c)),
                tok(2 * W), tok(LANES), tok(W),
            ],
            out_specs=tok(W),
            scratch_shapes=[
                pltpu.VMEM((H * T, LANES), F32), pltpu.VMEM((H * T, LANES), F32), pltpu.VMEM((H * T, HEAD_DIM), F32),
            ],
        ),
        out_shape=jax.ShapeDtypeStruct((B * T, W), F32),
        compiler_params=_cparams(("parallel", "arbitrary")),
        name="fox_sample_attn",
    )(page_table, q, *([kv_pool] * len(block_specs)), cum, kv_new, logf_new, zs)


def _alibi_slopes():
    s = np.exp2(-8.0 * np.arange(1, N_HEADS + 1) / N_HEADS)
    return jnp.asarray(s, dtype=F32).reshape(A_KV_GROUPS, A_REP)


_NSA_PLAN = (
    (0, 4, "normq", 0, (0,)),
    (4, 6, "raw", 0, (1,)),
    (6, 7, "norm", 2, (1, 2)),
    (7, 8, "raw", 0, (1, 2)),
    (8, 9, "norm", 3, (1, 2)),
    (9, 10, "raw", 0, (1, 2)),
    (10, 14, "silu", 0, (3,)),
)


def _nsa_outs(act_dtype):
    return ((0, 4, act_dtype), (4, 6, F32), (6, 4, act_dtype), (10, 4, act_dtype))


def _nsa_layer(xp, xs, norm_g, w_in, q_gain, k_gain, pe, w1, w2, w_out, slopes, past):
    mix = N_HEADS * HEAD_DIM
    kvw = A_KV_GROUPS * HEAD_DIM
    g0 = mix + 6 * kvw
    g1 = g0 + 3 * N_HEADS
    w_main = jnp.concatenate([w_in[:, :g0], w_in[:, g1:]], axis=1).astype(BF16)
    w_small = jnp.pad(w_in[:, g0:g1], ((0, 0), (0, LANES - 3 * N_HEADS))).astype(BF16)
    gains = jnp.concatenate([q_gain[None], k_gain, jnp.zeros((4, HEAD_DIM), F32)], axis=0)
    zero_bias = jnp.zeros((1, LANES), F32)
    w_out_bf = w_out.astype(BF16)
    slope_tab = jnp.broadcast_to(
        jnp.pad(slopes, ((0, 0), (0, 8 - A_REP)))[:, :, None], (A_KV_GROUPS, 8, LANES))
    shp = lambda B: (B, -1, 2, A_KV_GROUPS, HEAD_DIM)

    B, T, D = xp.shape
    M = B * T
    x2d = xp.reshape(M, D)
    q_bf, rows, rows_bf, zs, gates = _proj(x2d, norm_g, gains, zero_bias, w_main, w_small, _NSA_PLAN,
                                           _nsa_outs(BF16), "sigmoid", 512)
    rows3 = rows.reshape(B, T, -1)
    kvc = _compress_prompt(rows3, pe, w1.astype(BF16), w2.astype(BF16), k_gain[0])
    gates_g = gates[:, :3 * N_HEADS].reshape(B, T, 3, A_KV_GROUPS, A_REP)
    gates_g = jnp.transpose(gates_g, (0, 3, 1, 2, 4)).reshape(B, A_KV_GROUPS, T, 3 * A_REP)
    n_cmp = T // CMP_BLOCK
    expand = (jnp.arange(T, dtype=jnp.int32)[None, :] // SEL_BLOCK
              == jnp.arange(n_cmp, dtype=jnp.int32)[:, None]).astype(BF16)
    u = _nsa_prompt_attn(q_bf.reshape(B, T, -1), kvc, rows_bf.reshape(B, T, -1), gates_g,
                         zs.reshape(B, T, -1), slope_tab, expand)
    yp = _outproj(u.reshape(M, -1), w_out_bf, x2d, 512).reshape(B, T, D)
    out_p = (rows3[:, :, :2 * kvw].reshape(shp(B)), rows3[:, :, 2 * kvw:4 * kvw].reshape(shp(B)),
             rows3[:, T - min(WINDOW, T):, 4 * kvw:].reshape(shp(B)))

    cmp_pool, sel_pool, layer, page_table, win_buf = past
    B, T, D = xs.shape
    M = B * T
    x2d = xs.reshape(M, D)
    q, rows, _, zs, gates = _proj(x2d, norm_g, gains, zero_bias, w_main, w_small, _NSA_PLAN,
                                  _nsa_outs(F32), "sigmoid", M)
    kvc = _compress_paged(cmp_pool, layer, page_table, pe, w1, w2, k_gain[0])
    u = _nsa_sample_attn(q, kvc, sel_pool, layer, page_table, rows, win_buf, gates, zs, slope_tab, B, T)
    ys = _outproj(u, w_out_bf, x2d, M).reshape(B, T, D)
    rows3 = rows.reshape(B, T, -1)
    win_rows = rows3[:, :, 4 * kvw:].reshape(shp(B))
    out_s = (rows3[:, :, :2 * kvw].reshape(shp(B)), rows3[:, :, 2 * kvw:4 * kvw].reshape(shp(B)),
             jnp.concatenate([win_buf[:, T:], win_rows], axis=1))
    return yp, ys, out_p, out_s


_FOX_PLAN = (
    (0, 4, "normq", 0, (0,)),
    (4, 8, "norm", 1, (1, 2)),
    (8, 12, "raw", 0, (1, 2)),
    (12, 16, "silu", 0, (3,)),
)


def _fox_outs(act_dtype):
    return ((0, 4, act_dtype), (4, 8, F32), (4, 8, act_dtype), (12, 4, act_dtype))


def _fox_layer(xp, xs, norm_g, w_in, f_bias, q_gain, k_gain, w_out, past):
    mix = N_HEADS * HEAD_DIM
    w_main = jnp.concatenate([w_in[:, :3 * mix], w_in[:, 3 * mix + N_HEADS:]], axis=1).astype(BF16)
    w_small = jnp.pad(w_in[:, 3 * mix:3 * mix + N_HEADS], ((0, 0), (0, LANES - N_HEADS))).astype(BF16)
    gains = jnp.concatenate([q_gain[None], k_gain[None], jnp.zeros((6, HEAD_DIM), F32)], axis=0)
    sbias = jnp.pad(f_bias, (0, LANES - N_HEADS)).reshape(1, LANES)
    w_out_bf = w_out.astype(BF16)

    B, T, D = xp.shape
    M = B * T
    x2d = xp.reshape(M, D)
    q_bf, kv, kv_bf, zs, logf = _proj(x2d, norm_g, gains, sbias, w_main, w_small, _FOX_PLAN, _fox_outs(BF16),
                                      "logsig", 512)
    cum = _cumsum_heads(logf.reshape(B, T, LANES))[:, :N_HEADS].reshape(B, N_HEADS, 1, T)
    u = _fox_prompt_attn(q_bf.reshape(B, T, -1), kv_bf.reshape(B, T, -1), cum, zs.reshape(B, T, -1))
    yp = _outproj(u.reshape(M, -1), w_out_bf, x2d, 512).reshape(B, T, D)
    out_p = (kv.reshape(B, T, 2, N_HEADS, HEAD_DIM), logf[:, :N_HEADS].reshape(B, T, N_HEADS))

    kv_pool, logf_pool, layer, page_table = past
    B, T, D = xs.shape
    M = B * T
    x2d = xs.reshape(M, D)
    q, kv, _, zs, logf = _proj(x2d, norm_g, gains, sbias, w_main, w_small, _FOX_PLAN, _fox_outs(F32),
                               "logsig", M)
    cum = _cumsum_paged(logf_pool, layer, page_table)
    u = _fox_sample_attn(q, kv_pool, layer, page_table, cum, kv, logf, zs, B, T)
    ys = _outproj(u, w_out_bf, x2d, M).reshape(B, T, D)
    out_s = (kv.reshape(B, T, 2, N_HEADS, HEAD_DIM), logf[:, :N_HEADS].reshape(B, T, N_HEADS))
    return yp, ys, out_p, out_s


def kernel(x_prompt, x_sample, cache_a_cmp, cache_a_sel, state_a_win, cache_b_kv, cache_b_logf, page_table,
           a_norm, a_w_in, a_q_gain, a_k_gain, a_phi_pe, a_phi_w1, a_phi_w2, a_w_out,
           b_norm, b_w_in, b_f_bias, b_q_gain, b_k_gain, b_w_out):
    slopes = _alibi_slopes()
    xp, xs = x_prompt, x_sample
    depth = a_norm.shape[0] + b_norm.shape[0]
    a_p, a_s, b_p, b_s = [], [], [], []
    for i in range(depth):
        j = i // 2
        if i % 2 == 0:
            xp, xs, out_p, out_s = _nsa_layer(
                xp, xs, a_norm[j], a_w_in[j], a_q_gain[j], a_k_gain[j], a_phi_pe[j], a_phi_w1[j], a_phi_w2[j],
                a_w_out[j], slopes, (cache_a_cmp, cache_a_sel, j, page_table, state_a_win[j]))
            a_p.append(out_p)
            a_s.append(out_s)
        else:
            xp, xs, out_p, out_s = _fox_layer(
                xp, xs, b_norm[j], b_w_in[j], b_f_bias[j], b_q_gain[j], b_k_gain[j], b_w_out[j],
                (cache_b_kv, cache_b_logf, j, page_table))
            b_p.append(out_p)
            b_s.append(out_s)
    stack = lambda outs, k: jnp.stack([o[k] for o in outs])
    return (xp, xs,
            stack(a_p, 0), stack(a_s, 0), stack(a_p, 1), stack(a_s, 1), stack(a_p, 2), stack(a_s, 2),
            stack(b_p, 0), stack(b_s, 0), stack(b_p, 1), stack(b_s, 1))
```

```python
import functools

import jax
import jax.numpy as jnp
import numpy as np
from jax import lax
from jax.experimental import pallas as pl
from jax.experimental.pallas import tpu as pltpu

HEAD_DIM = 128
N_HEADS = 16
A_KV_GROUPS = 4
A_REP = N_HEADS // A_KV_GROUPS
CMP_BLOCK = 64
SEL_BLOCK = CMP_BLOCK
TOP_N = 16
WINDOW = 512
RMS_EPS = 1e-6
SCALE = HEAD_DIM ** -0.5
FORCED_SCORE = float(A_REP + 1)
NEG_BIG = -1e30

LANES = 128
SUBLANES = 8
PROJ_COLS = 512
VMEM_LIMIT = 56 * 1024 * 1024
CMP_PAGES = 16
SEL_PAGES = 8
FOX_PAGES = 4
LOGF_PAGES = 8
ROW_CHUNK = 128

F32 = jnp.float32
BF16 = jnp.bfloat16
NT = (((1,), (1,)), ((), ()))
TN = (((0,), (0,)), ((), ()))


def _cparams(sem):
    return pltpu.CompilerParams(dimension_semantics=sem, vmem_limit_bytes=VMEM_LIMIT)


def _head_rms(a, gain):
    ms = jnp.mean(a * a, axis=-1, keepdims=True)
    return a * lax.rsqrt(ms + RMS_EPS) * gain


def _softmax_rows(s, ok):
    m = jnp.max(s, axis=-1, keepdims=True)
    p = jnp.where(ok, jnp.exp(s - m), 0.0)
    return p / jnp.maximum(jnp.sum(p, axis=-1, keepdims=True), 1e-30)


def _split3(x):
    hi = x.astype(BF16)
    r1 = x - hi.astype(F32)
    mid = r1.astype(BF16)
    lo = (r1 - mid.astype(F32)).astype(BF16)
    return hi, mid, lo


def _lower_ones(n):
    tri = lax.broadcasted_iota(jnp.int32, (n, n), 0) >= lax.broadcasted_iota(jnp.int32, (n, n), 1)
    return jnp.where(tri, 1.0, 0.0).astype(BF16)


def _tile_row(block_ref, j):
    n, _, d = block_ref.shape
    return block_ref.reshape(n * SUBLANES, d)[pl.ds(j, n, stride=SUBLANES), :]


def _rank_counts(score, blk, n_blk):
    rank = jnp.zeros(score.shape, F32)
    for i in range(n_blk):
        col = score[:, i:i + 1]
        rank = rank + jnp.where(blk > i, jnp.where(col >= score, 1.0, 0.0), jnp.where(col > score, 1.0, 0.0))
    return rank


def _proj_kernel(plan, small_kind, x_ref, ng_ref, gains_ref, sbias_ref, w_ref, ws_ref, *refs):
    n_out = len(refs) - 1
    outs, xn_ref = refs[:n_out], refs[n_out]
    j = pl.program_id(1)

    @pl.when(j == 0)
    def _():
        x = x_ref[...]
        ms = jnp.mean(x * x, axis=-1, keepdims=True)
        xn_ref[...] = (x * lax.rsqrt(ms + RMS_EPS) * ng_ref[...]).astype(BF16)
        small = jnp.dot(xn_ref[...], ws_ref[...], preferred_element_type=F32)
        if small_kind == "sigmoid":
            small = jax.nn.sigmoid(small)
        else:
            small = small + sbias_ref[...]
            small = jnp.minimum(small, 0.0) - jnp.log1p(jnp.exp(-jnp.abs(small)))
        outs[n_out - 1][...] = small

    for (j0, j1, kind, gain_row, targets) in plan:
        @pl.when((j >= j0) & (j < j1))
        def _(kind=kind, gain_row=gain_row, targets=targets):
            acc = jnp.dot(xn_ref[...], w_ref[...], preferred_element_type=F32)
            if kind in ("norm", "normq"):
                g = gains_ref[gain_row:gain_row + 1, :]
                if kind == "normq":
                    g = g * SCALE
                acc = jnp.concatenate(
                    [_head_rms(acc[:, h * LANES:(h + 1) * LANES], g) for h in range(PROJ_COLS // LANES)], axis=1)
            elif kind == "silu":
                acc = acc * jax.nn.sigmoid(acc)
            for t in targets:
                outs[t][...] = acc.astype(outs[t].dtype)


def _proj(x2d, norm_gain, gains, sbias, w_main, w_small, plan, out_defs, small_kind, bm):
    M, D = x2d.shape
    nblk = w_main.shape[1] // PROJ_COLS
    grid = (M // bm, nblk)

    def out_map(j0, n):
        return lambda i, j: (i, jnp.clip(j - j0, 0, n - 1))

    out_shapes = [jax.ShapeDtypeStruct((M, n * PROJ_COLS), dt) for (_, n, dt) in out_defs]
    out_specs = [pl.BlockSpec((bm, PROJ_COLS), out_map(j0, n)) for (j0, n, _) in out_defs]
    out_shapes.append(jax.ShapeDtypeStruct((M, LANES), F32))
    out_specs.append(pl.BlockSpec((bm, LANES), lambda i, j: (i, 0)))
    return pl.pallas_call(
        functools.partial(_proj_kernel, plan, small_kind),
        grid=grid,
        in_specs=[
            pl.BlockSpec((bm, D), lambda i, j: (i, 0)),
            pl.BlockSpec((1, D), lambda i, j: (0, 0)),
            pl.BlockSpec((8, LANES), lambda i, j: (0, 0)),
            pl.BlockSpec((1, LANES), lambda i, j: (0, 0)),
            pl.BlockSpec((D, PROJ_COLS), lambda i, j: (0, j)),
            pl.BlockSpec((D, LANES), lambda i, j: (0, 0)),
        ],
        out_specs=out_specs,
        out_shape=out_shapes,
        scratch_shapes=[pltpu.VMEM((bm, D), BF16)],
        compiler_params=_cparams(("parallel", "arbitrary")),
        name="proj",
    )(x2d, norm_gain.reshape(1, D), gains, sbias, w_main, w_small)


def _outproj_kernel(u_ref, w_ref, x_ref, o_ref):
    o_ref[...] = x_ref[...] + jnp.dot(u_ref[...].astype(BF16), w_ref[...], preferred_element_type=F32)


def _outproj(u2d, w_bf, x2d, bm):
    M, K = u2d.shape
    N = w_bf.shape[1]
    return pl.pallas_call(
        _outproj_kernel,
        grid=(M // bm,),
        in_specs=[
            pl.BlockSpec((bm, K), lambda i: (i, 0)),
            pl.BlockSpec((K, N), lambda i: (0, 0)),
            pl.BlockSpec((bm, N), lambda i: (i, 0)),
        ],
        out_specs=pl.BlockSpec((bm, N), lambda i: (i, 0)),
        out_shape=jax.ShapeDtypeStruct((M, N), F32),
        compiler_params=_cparams(("parallel",)),
        name="outproj",
    )(u2d, w_bf, x2d)


def _compress_kernel(n_blk, x0_ref, x1_ref, x2_ref, x3_ref, pe_ref, w1_ref, w2_ref, kg_ref, o_ref):
    kv = pl.program_id(1)

    def body(i, acc):
        pe = pe_ref[pl.ds(i, 1), :]
        a = jnp.concatenate(
            [x_ref[pl.ds(i, n_blk, stride=CMP_BLOCK), :] + pe for x_ref in (x0_ref, x1_ref, x2_ref, x3_ref)],
            axis=0)
        return acc + jnp.dot(a.astype(BF16), w1_ref[i], preferred_element_type=F32)

    acc = lax.fori_loop(0, CMP_BLOCK, body, jnp.zeros((A_KV_GROUPS * n_blk, HEAD_DIM), F32))
    h = jax.nn.gelu(acc)
    out = jnp.dot(h.astype(BF16), w2_ref[...], preferred_element_type=F32)
    out = jnp.where(kv == 0, _head_rms(out, kg_ref[...]), out)
    for g in range(A_KV_GROUPS):
        o_ref[g] = out[g * n_blk:(g + 1) * n_blk].astype(o_ref.dtype)


def _compress_prompt(rows, pe, w1_bf, w2_bf, k_gain_cmp):
    B, T = rows.shape[0], rows.shape[1]
    n_blk = T // CMP_BLOCK
    G = A_KV_GROUPS
    grp = lambda g: pl.BlockSpec((None, T, HEAD_DIM), lambda b, kv, g=g: (b, 0, kv * G + g))
    return pl.pallas_call(
        functools.partial(_compress_kernel, n_blk),
        grid=(B, 2),
        in_specs=[
            grp(0), grp(1), grp(2), grp(3),
            pl.BlockSpec((None, CMP_BLOCK, HEAD_DIM), lambda b, kv: (kv, 0, 0)),
            pl.BlockSpec((None, CMP_BLOCK, HEAD_DIM, HEAD_DIM), lambda b, kv: (kv, 0, 0, 0)),
            pl.BlockSpec((None, HEAD_DIM, HEAD_DIM), lambda b, kv: (kv, 0, 0)),
            pl.BlockSpec((1, HEAD_DIM), lambda b, kv: (0, 0)),
        ],
        out_specs=pl.BlockSpec((None, None, A_KV_GROUPS, n_blk, HEAD_DIM), lambda b, kv: (b, kv, 0, 0, 0)),
        out_shape=jax.ShapeDtypeStruct((B, 2, A_KV_GROUPS, n_blk, HEAD_DIM), BF16),
        compiler_params=_cparams(("parallel", "arbitrary")),
        name="compress",
    )(rows, rows, rows, rows, pe, w1_bf, w2_bf, k_gain_cmp.reshape(1, HEAD_DIM))


def _compress_paged_kernel(n_pg, bpp, pt_ref, *refs):
    pages = refs[:n_pg]
    pe_ref, w1_ref, w2_ref, kg_ref, o_ref = refs[n_pg:]
    n_rows = n_pg * bpp * SUBLANES
    slabs = []
    for i in range(CMP_BLOCK):
        pe = pe_ref[i]
        a = jnp.concatenate([pg[h * CMP_BLOCK + i] + pe for pg in pages for h in range(bpp)], axis=0)
        slabs.append(a.astype(BF16))
    acc = jnp.dot(jnp.concatenate(slabs, axis=1), w1_ref[...], preferred_element_type=F32)
    h = jax.nn.gelu(acc)
    out_k = jnp.dot(h[:, :HEAD_DIM].astype(BF16), w2_ref[0], preferred_element_type=F32)
    out_v = jnp.dot(h[:, HEAD_DIM:].astype(BF16), w2_ref[1], preferred_element_type=F32)
    out_k = _head_rms(out_k, kg_ref[...])
    is_k = (lax.broadcasted_iota(jnp.int32, (n_rows, 1), 0) & (SUBLANES - 1)) < A_KV_GROUPS
    o_ref[...] = jnp.where(is_k, out_k, out_v)


def _compress_paged(pool, layer, page_table, pe, w1, w2, k_gain_cmp):
    page = pool.shape[2]
    bpp = page // CMP_BLOCK
    B, n_pages = page_table.shape
    n_pg = CMP_PAGES
    assert page % CMP_BLOCK == 0 and n_pages % n_pg == 0 and 2 * A_KV_GROUPS == SUBLANES
    pool5 = pool.reshape(pool.shape[0], pool.shape[1], page, SUBLANES, HEAD_DIM)
    pe8 = jnp.repeat(jnp.transpose(pe, (1, 0, 2)), A_KV_GROUPS, axis=1)
    w1cat = jnp.concatenate([w1[0], w1[1]], axis=-1).astype(BF16)
    w1cat = w1cat.reshape(CMP_BLOCK * HEAD_DIM, 2 * HEAD_DIM)
    rows_per_step = n_pg * bpp * SUBLANES
    page_spec = lambda p: pl.BlockSpec(
        (None, None, page, SUBLANES, HEAD_DIM), lambda b, c, pt, p=p: (layer, pt[b, c * n_pg + p], 0, 0, 0))
    return pl.pallas_call(
        functools.partial(_compress_paged_kernel, n_pg, bpp),
        grid_spec=pltpu.PrefetchScalarGridSpec(
            num_scalar_prefetch=1,
            grid=(B, n_pages // n_pg),
            in_specs=[page_spec(p) for p in range(n_pg)] + [
                pl.BlockSpec((CMP_BLOCK, SUBLANES, HEAD_DIM), lambda b, c, pt: (0, 0, 0)),
                pl.BlockSpec((CMP_BLOCK * HEAD_DIM, 2 * HEAD_DIM), lambda b, c, pt: (0, 0)),
                pl.BlockSpec((2, HEAD_DIM, HEAD_DIM), lambda b, c, pt: (0, 0, 0)),
                pl.BlockSpec((1, HEAD_DIM), lambda b, c, pt: (0, 0)),
            ],
            out_specs=pl.BlockSpec((None, rows_per_step, HEAD_DIM), lambda b, c, pt: (b, c, 0)),
        ),
        out_shape=jax.ShapeDtypeStruct((B, n_pages * bpp * SUBLANES, HEAD_DIM), F32),
        compiler_params=_cparams(("parallel", "arbitrary")),
        name="compress_paged",
    )(page_table, *([pool5] * n_pg), pe8, w1cat, w2.astype(BF16), k_gain_cmp.reshape(1, HEAD_DIM))


def _select_blocks_t(imp_t, cur, n_blk):
    blk = lax.broadcasted_iota(jnp.int32, (n_blk, 1), 0)
    valid = blk <= cur
    forced = (blk == 0) | (blk == cur) | (blk == cur - 1)
    score = jnp.where(valid, jnp.where(forced, FORCED_SCORE, imp_t), -jnp.inf)
    sub = lax.broadcasted_iota(jnp.int32, (SUBLANES, 1), 0)
    ranks = []
    for b0 in range(0, n_blk, SUBLANES):
        slab = score[b0:b0 + SUBLANES]
        rank = jnp.zeros(slab.shape, F32)
        for i in range(n_blk):
            row = score[i:i + 1]
            if i < b0:
                rank = rank + jnp.where(row >= slab, 1.0, 0.0)
            elif i >= b0 + SUBLANES:
                rank = rank + jnp.where(row > slab, 1.0, 0.0)
            else:
                rank = rank + jnp.where(sub > i - b0, jnp.where(row >= slab, 1.0, 0.0),
                                        jnp.where(row > slab, 1.0, 0.0))
        ranks.append(rank)
    rank = jnp.concatenate(ranks, axis=0)
    return jnp.where(valid & (rank < float(min(TOP_N, n_blk))), 1.0, 0.0)


def _nsa_prompt_kernel(tq, tk, T, n_full, q_ref, kc_ref, vc_ref, ks_ref, vs_ref, kw_ref, vw_ref, gt_ref, zs_ref,
                       slope_ref, exp_ref, o_ref):
    qi = n_full * (tk // tq) + pl.program_id(2)
    t0 = qi * tq
    n_cmp = T // CMP_BLOCK
    R = A_REP
    q = q_ref[...]
    q4 = jnp.concatenate([q[:, r * HEAD_DIM:(r + 1) * HEAD_DIM] for r in range(R)], axis=0)
    qpos = t0 + lax.broadcasted_iota(jnp.int32, (tq, 1), 0)
    qpos_row = t0 + lax.broadcasted_iota(jnp.int32, (1, tq), 1)
    slopes = [slope_ref[r:r + 1, 0:1] for r in range(R)]

    sct = lax.dot_general(kc_ref[...], q4, NT, preferred_element_type=F32)
    c_end = (lax.broadcasted_iota(jnp.int32, (n_cmp, 1), 0) + 1) * CMP_BLOCK - 1
    ok_c = qpos_row >= c_end
    cposf = (c_end - t0).astype(F32)
    p_parts = []
    for r in range(R):
        s_r = jnp.where(ok_c, sct[:, r * tq:(r + 1) * tq] + slopes[r] * cposf, NEG_BIG)
        p_r = jnp.where(ok_c, jnp.exp(s_r - jnp.max(s_r, axis=0, keepdims=True)), 0.0)
        p_parts.append(p_r / jnp.maximum(jnp.sum(p_r, axis=0, keepdims=True), 1e-30))
    p_ct = jnp.concatenate(p_parts, axis=1).astype(BF16)
    o_cmp = lax.dot_general(p_ct, vc_ref[...], TN, preferred_element_type=F32)
    imp_t = p_parts[0]
    for r in range(1, R):
        imp_t = imp_t + p_parts[r]
    sel_t = _select_blocks_t(imp_t, qpos_row // SEL_BLOCK, n_cmp).astype(BF16)

    def sel_tile(kt, carry, diagonal):
        m, l, acc = carry
        k0 = kt * tk
        s = lax.dot_general(q4, ks_ref[k0:k0 + tk, :], NT, preferred_element_type=F32)
        v = vs_ref[k0:k0 + tk, :]
        kpos = k0 + lax.broadcasted_iota(jnp.int32, (1, tk), 1)
        picked = lax.dot_general(sel_t, exp_ref[:, k0:k0 + tk], TN, preferred_element_type=F32)
        mask = (picked - 1.0) * (-NEG_BIG)
        if diagonal:
            mask = jnp.where(kpos <= qpos, mask, NEG_BIG)
        kposf = (kpos - t0).astype(F32)
        s = jnp.concatenate(
            [s[r * tq:(r + 1) * tq] + (mask + slopes[r] * kposf) for r in range(R)], axis=0)
        m_new = jnp.maximum(m, jnp.max(s, axis=-1, keepdims=True))
        alpha = jnp.exp(m - m_new)
        p = jnp.exp(s - m_new)
        l = alpha * l + jnp.sum(p, axis=-1, keepdims=True)
        acc = alpha * acc + jnp.dot(p.astype(BF16), v, preferred_element_type=F32)
        return m_new, l, acc

    carry = (jnp.full((R * tq, 1), NEG_BIG, F32), jnp.zeros((R * tq, 1), F32), jnp.zeros((R * tq, HEAD_DIM), F32))
    for kt in range(n_full):
        carry = sel_tile(kt, carry, False)
    _, l_s, acc_s = sel_tile(n_full, carry, True)
    o_sel = acc_s / l_s

    span = WINDOW + tq
    w0 = pl.multiple_of(jnp.maximum(t0 - WINDOW, 0), tq)
    kw = kw_ref[pl.ds(w0, span), :]
    vw = vw_ref[pl.ds(w0, span), :]
    sw = lax.dot_general(q4, kw, NT, preferred_element_type=F32)
    wpos = w0 + lax.broadcasted_iota(jnp.int32, (1, span), 1)
    d_w = qpos - wpos
    mask_w = jnp.where((d_w >= 0) & (d_w < WINDOW), 0.0, NEG_BIG)
    wposf = (wpos - t0).astype(F32)
    sw = jnp.concatenate(
        [sw[r * tq:(r + 1) * tq] + (mask_w + slopes[r] * wposf) for r in range(R)], axis=0)
    pw = jnp.exp(sw - jnp.max(sw, axis=-1, keepdims=True))
    o_win = (jnp.dot(pw.astype(BF16), vw, preferred_element_type=F32)
             / jnp.sum(pw, axis=-1, keepdims=True))

    gt = gt_ref[...]
    zs = zs_ref[...]
    outs = []
    for r in range(R):
        rows = slice(r * tq, (r + 1) * tq)
        o_r = (gt[:, r:r + 1] * o_cmp[rows] + gt[:, R + r:R + r + 1] * o_sel[rows]
               + gt[:, 2 * R + r:2 * R + r + 1] * o_win[rows])
        outs.append(o_r * zs[:, r * HEAD_DIM:(r + 1) * HEAD_DIM].astype(F32))
    o_ref[...] = jnp.concatenate(outs, axis=1).astype(o_ref.dtype)


def _nsa_prompt_attn(q_bf, kvc, rows_bf, gates_g, zs, slope_tab, expand, tq=256, tk=512):
    B, T, W = q_bf.shape
    G = A_KV_GROUPS
    gw = A_REP * HEAD_DIM
    n_cmp = T // CMP_BLOCK
    tk = min(tk, T)
    assert T % tk == 0 and tk % tq == 0 and T >= WINDOW + tq and tk >= WINDOW
    per_call = tk // tq
    outs = []
    for n_full in range(T // tk):
        n_keys = (n_full + 1) * tk
        q0 = n_full * per_call
        n_win = max(n_keys, WINDOW + tq)
        prefix = lambda col, rows=n_keys: pl.BlockSpec(
            (None, rows, HEAD_DIM), lambda b, g, i, col=col: (b, 0, col * G + g))
        tile = lambda b, g, i, q0=q0: (b, q0 + i, g)
        outs.append(pl.pallas_call(
            functools.partial(_nsa_prompt_kernel, tq, tk, T, n_full),
            grid=(B, G, per_call),
            in_specs=[
                pl.BlockSpec((None, tq, gw), tile),
                pl.BlockSpec((None, None, None, n_cmp, HEAD_DIM), lambda b, g, i: (b, 0, g, 0, 0)),
                pl.BlockSpec((None, None, None, n_cmp, HEAD_DIM), lambda b, g, i: (b, 1, g, 0, 0)),
                prefix(0), prefix(1), prefix(2, n_win), prefix(3, n_win),
                pl.BlockSpec((None, None, tq, 3 * A_REP), lambda b, g, i, q0=q0: (b, g, q0 + i, 0)),
                pl.BlockSpec((None, tq, gw), tile),
                pl.BlockSpec((None, 8, LANES), lambda b, g, i: (g, 0, 0)),
                pl.BlockSpec((n_cmp, n_keys), lambda b, g, i: (0, 0)),
            ],
            out_specs=pl.BlockSpec((None, tq, gw), lambda b, g, i: (b, i, g)),
            out_shape=jax.ShapeDtypeStruct((B, tk, W), BF16),
            compiler_params=_cparams(("parallel", "parallel", "arbitrary")),
            name="nsa_prompt_attn",
        )(q_bf, kvc, kvc, rows_bf, rows_bf, rows_bf, rows_bf, gates_g, zs, slope_tab, expand))
    return jnp.concatenate(outs, axis=1)


def _nsa_sample_kernel(n_pg, page, P_len, T, wb, pt_ref, q_ref, kvc_ref, *refs):
    pages = refs[:n_pg]
    rows_ref, win_ref, gt_ref, zs_ref, slope_ref, o_ref, m_ref, l_ref, acc_ref, sel_ref, ocmp_ref = refs[n_pg:]
    c = pl.program_id(1)
    G, R = A_KV_GROUPS, A_REP
    n_blk = P_len // SEL_BLOCK
    q = q_ref[...]
    tpos = lax.broadcasted_iota(jnp.int32, (T, 1), 0)

    def q4(g):
        return jnp.concatenate(
            [q[:, (g * R + r) * HEAD_DIM:(g * R + r + 1) * HEAD_DIM] for r in range(R)], axis=0).astype(BF16)

    def slope(g, r):
        return slope_ref[g, r:r + 1, 0:1]

    def flash_update(g, k, v, ok, kposf):
        s = lax.dot_general(q4(g), k, NT, preferred_element_type=F32)
        s = jnp.concatenate(
            [jnp.where(ok, s[r * T:(r + 1) * T] + slope(g, r) * kposf, NEG_BIG) for r in range(R)], axis=0)
        ok4 = jnp.concatenate([ok] * R, axis=0)
        m = m_ref[g][:, 0:1]
        m_new = jnp.maximum(m, jnp.max(s, axis=-1, keepdims=True))
        alpha = jnp.exp(m - m_new)
        p = jnp.where(ok4, jnp.exp(s - m_new), 0.0)
        l_new = alpha * l_ref[g][:, 0:1] + jnp.sum(p, axis=-1, keepdims=True)
        acc_ref[g] = alpha * acc_ref[g] + jnp.dot(p.astype(BF16), v, preferred_element_type=F32)
        m_ref[g] = jnp.broadcast_to(m_new, (R * T, LANES))
        l_ref[g] = jnp.broadcast_to(l_new, (R * T, LANES))

    @pl.when(c == 0)
    def _():
        c_end = (lax.broadcasted_iota(jnp.int32, (1, n_blk), 1) + 1) * CMP_BLOCK - 1
        cposf = (c_end - P_len).astype(F32)
        ok_c = (P_len + tpos) >= c_end
        imps = []
        for g in range(G):
            kc = kvc_ref[pl.ds(g, n_blk, stride=SUBLANES), :].astype(BF16)
            vc = kvc_ref[pl.ds(G + g, n_blk, stride=SUBLANES), :].astype(BF16)
            sc = lax.dot_general(q4(g), kc, NT, preferred_element_type=F32)
            parts = []
            for r in range(R):
                s_r = sc[r * T:(r + 1) * T] + slope(g, r) * cposf
                parts.append(_softmax_rows(jnp.where(ok_c, s_r, NEG_BIG), ok_c))
            ocmp_ref[g] = jnp.dot(jnp.concatenate(parts, axis=0).astype(BF16), vc, preferred_element_type=F32)
            imps.append(parts[0] + parts[1] + parts[2] + parts[3])
        imp = jnp.concatenate(imps, axis=0)
        blk = lax.broadcasted_iota(jnp.int32, (1, n_blk), 1)
        forced = (blk == 0) | (blk == n_blk - 1)
        score = jnp.where(forced, FORCED_SCORE, imp)
        rank = _rank_counts(score, blk, n_blk) + jnp.where(FORCED_SCORE > score, 1.0, 0.0)
        sel_ref[...] = jnp.where(rank < float(TOP_N), 1.0, 0.0)
        m_ref[...] = jnp.full(m_ref.shape, NEG_BIG, F32)
        l_ref[...] = jnp.zeros(l_ref.shape, F32)
        acc_ref[...] = jnp.zeros(acc_ref.shape, F32)

    n_keys = n_pg * page
    kpos = c * n_keys + lax.broadcasted_iota(jnp.int32, (1, n_keys), 1)
    kposf = (kpos - P_len).astype(F32)
    rowblk = lax.broadcasted_iota(jnp.int32, (n_blk, 1), 0)
    expand = jnp.where(rowblk == kpos // SEL_BLOCK, 1.0, 0.0).astype(BF16)
    picked = jnp.dot(sel_ref[...].astype(BF16), expand, preferred_element_type=F32)
    for g in range(G):
        k = jnp.concatenate([pg[pl.ds(g, page, stride=SUBLANES), :] for pg in pages], axis=0).astype(BF16)
        v = jnp.concatenate([pg[pl.ds(G + g, page, stride=SUBLANES), :] for pg in pages], axis=0).astype(BF16)
        flash_update(g, k, v, picked[g * T:(g + 1) * T] > 0.5, kposf)

    @pl.when(c == pl.num_programs(1) - 1)
    def _():
        kvw = G * HEAD_DIM
        rows = rows_ref[...]
        gt = gt_ref[...]
        zs = zs_ref[...]
        pad = jnp.zeros((LANES - T, HEAD_DIM), F32)
        lane = lax.broadcasted_iota(jnp.int32, (1, LANES), 1)
        widx = lax.broadcasted_iota(jnp.int32, (1, wb + LANES), 1)
        ok_w = (widx > tpos + (wb - WINDOW)) & (widx <= tpos + wb)
        wposf = (widx - wb).astype(F32)
        outs = []
        for g in range(G):
            new = lambda sec: jnp.concatenate(
                [rows[:, sec * kvw + g * HEAD_DIM:sec * kvw + (g + 1) * HEAD_DIM], pad], axis=0)
            flash_update(g, new(2).astype(BF16), new(3).astype(BF16), lane <= tpos, lane.astype(F32))
            o_sel = acc_ref[g] / jnp.maximum(l_ref[g][:, 0:1], 1e-30)
            kw = jnp.concatenate([win_ref[pl.ds(g, wb, stride=SUBLANES), :], new(4)], axis=0).astype(BF16)
            vw = jnp.concatenate([win_ref[pl.ds(G + g, wb, stride=SUBLANES), :], new(5)], axis=0).astype(BF16)
            sw = lax.dot_general(q4(g), kw, NT, preferred_element_type=F32)
            parts = []
            for r in range(R):
                s_r = sw[r * T:(r + 1) * T] + slope(g, r) * wposf
                parts.append(_softmax_rows(jnp.where(ok_w, s_r, NEG_BIG), ok_w))
            o_win = jnp.dot(jnp.concatenate(parts, axis=0).astype(BF16), vw, preferred_element_type=F32)
            o_cmp = ocmp_ref[g]
            for r in range(R):
                h = g * R + r
                rs = slice(r * T, (r + 1) * T)
                o_r = (gt[:, h:h + 1] * o_cmp[rs] + gt[:, N_HEADS + h:N_HEADS + h + 1] * o_sel[rs]
                       + gt[:, 2 * N_HEADS + h:2 * N_HEADS + h + 1] * o_win[rs])
                outs.append(o_r * zs[:, h * HEAD_DIM:(h + 1) * HEAD_DIM])
        o_ref[...] = jnp.concatenate(outs, axis=1)


def _nsa_sample_attn(q, kvc, sel_pool, layer, page_table, rows, win_buf, gates, zs, slope_tab, B, T):
    page = sel_pool.shape[2]
    n_pages = page_table.shape[1]
    P_len = n_pages * page
    wb = win_buf.shape[1]
    n_pg = SEL_PAGES
    n_blk = P_len // SEL_BLOCK
    G, R = A_KV_GROUPS, A_REP
    W = N_HEADS * HEAD_DIM
    assert P_len % SEL_BLOCK == 0 and T <= SEL_BLOCK and T % SUBLANES == 0 and n_blk + 1 > TOP_N
    assert n_pages % n_pg == 0 and WINDOW <= wb <= P_len and T <= LANES
    pool4 = sel_pool.reshape(sel_pool.shape[0], sel_pool.shape[1], page * SUBLANES, HEAD_DIM)
    win3 = win_buf.reshape(B, wb * SUBLANES, HEAD_DIM)
    page_spec = lambda p: pl.BlockSpec(
        (None, None, page * SUBLANES, HEAD_DIM), lambda b, c, pt, p=p: (layer, pt[b, c * n_pg + p], 0, 0))
    tok = lambda width: pl.BlockSpec((T, width), lambda b, c, pt: (b, 0))
    return pl.pallas_call(
        functools.partial(_nsa_sample_kernel, n_pg, page, P_len, T, wb),
        grid_spec=pltpu.PrefetchScalarGridSpec(
            num_scalar_prefetch=1,
            grid=(B, n_pages // n_pg),
            in_specs=[tok(W), pl.BlockSpec((None, n_blk * SUBLANES, HEAD_DIM), lambda b, c, pt: (b, 0, 0))]
            + [page_spec(p) for p in range(n_pg)] + [
                tok(rows.shape[1]),
                pl.BlockSpec((None, wb * SUBLANES, HEAD_DIM), lambda b, c, pt: (b, 0, 0)),
                tok(LANES), tok(W),
                pl.BlockSpec((G, 8, LANES), lambda b, c, pt: (0, 0, 0)),
            ],
            out_specs=tok(W),
            scratch_shapes=[
                pltpu.VMEM((G, R * T, LANES), F32), pltpu.VMEM((G, R * T, LANES), F32),
                pltpu.VMEM((G, R * T, HEAD_DIM), F32), pltpu.VMEM((G * T, n_blk), F32),
                pltpu.VMEM((G, R * T, HEAD_DIM), F32),
            ],
        ),
        out_shape=jax.ShapeDtypeStruct((B * T, W), F32),
        compiler_params=_cparams(("parallel", "arbitrary")),
        name="nsa_sample_attn",
    )(page_table, q, kvc, *([pool4] * n_pg), rows, win3, gates, zs, slope_tab)


def _cumsum_kernel(tb, lf_ref, o_ref, carry_ref):
    @pl.when(pl.program_id(1) == 0)
    def _():
        carry_ref[...] = jnp.zeros_like(carry_ref)

    lower = _lower_ones(tb)
    acc = jnp.zeros((tb, LANES), F32)
    for part in _split3(lf_ref[...]):
        acc = acc + jnp.dot(lower, part, preferred_element_type=F32)
    acc = acc.T + carry_ref[...]
    o_ref[...] = acc
    carry_ref[...] = acc[:, tb - 1:tb]


def _cumsum_heads(logf, tb=256):
    B, T, _ = logf.shape
    tb = min(tb, T)
    return pl.pallas_call(
        functools.partial(_cumsum_kernel, tb),
        grid=(B, T // tb),
        in_specs=[pl.BlockSpec((None, tb, LANES), lambda b, i: (b, i, 0))],
        out_specs=pl.BlockSpec((None, LANES, tb), lambda b, i: (b, 0, i)),
        out_shape=jax.ShapeDtypeStruct((B, LANES, T), F32),
        scratch_shapes=[pltpu.VMEM((LANES, 1), F32)],
        compiler_params=_cparams(("parallel", "arbitrary")),
        name="fox_cumsum",
    )(logf)


def _cumsum_paged_kernel(n_pg, page, H, pt_ref, *refs):
    pages = refs[:n_pg]
    o_ref, carry_ref = refs[n_pg:]

    @pl.when(pl.program_id(1) == 0)
    def _():
        carry_ref[...] = jnp.zeros_like(carry_ref)

    lower = _lower_ones(page)
    widen = (lax.broadcasted_iota(jnp.int32, (H, LANES), 0) == lax.broadcasted_iota(jnp.int32, (H, LANES), 1))
    widen = jnp.where(widen, 1.0, 0.0).astype(BF16)
    carry = carry_ref[...]
    for p, pg in enumerate(pages):
        acc = jnp.zeros((page, LANES), F32)
        for part in _split3(pg[...]):
            wide = jnp.dot(part, widen, preferred_element_type=F32).astype(BF16)
            acc = acc + jnp.dot(lower, wide, preferred_element_type=F32)
        cum = acc.T + carry
        o_ref[:, p * page:(p + 1) * page] = cum
        carry = cum[:, page - 1:page]
    carry_ref[...] = carry


def _cumsum_paged(pool, layer, page_table):
    page, H = pool.shape[2], pool.shape[3]
    B, n_pages = page_table.shape
    n_pg = LOGF_PAGES
    assert n_pages % n_pg == 0 and page % LANES == 0
    page_spec = lambda p: pl.BlockSpec(
        (None, None, page, H), lambda b, c, pt, p=p: (layer, pt[b, c * n_pg + p], 0, 0))
    return pl.pallas_call(
        functools.partial(_cumsum_paged_kernel, n_pg, page, H),
        grid_spec=pltpu.PrefetchScalarGridSpec(
            num_scalar_prefetch=1,
            grid=(B, n_pages // n_pg),
            in_specs=[page_spec(p) for p in range(n_pg)],
            out_specs=pl.BlockSpec((None, LANES, n_pg * page), lambda b, c, pt: (b, 0, c)),
            scratch_shapes=[pltpu.VMEM((LANES, 1), F32)],
        ),
        out_shape=jax.ShapeDtypeStruct((B, LANES, n_pages * page), F32),
        compiler_params=_cparams(("parallel", "arbitrary")),
        name="fox_cumsum_paged",
    )(page_table, *([pool] * n_pg))


def _fox_prompt_kernel(tq, tk, qi, q_ref, k_ref, v_ref, cum_ref, zs_ref, o_ref):
    t0 = qi * tq
    q = q_ref[...]
    qpos = t0 + lax.broadcasted_iota(jnp.int32, (tq, 1), 0)

    def step(kt, carry, masked):
        m, l, acc = carry
        k0 = kt * tk
        v = v_ref[k0:k0 + tk, :]
        s = (lax.dot_general(q, k_ref[k0:k0 + tk, :], NT, preferred_element_type=F32)
             - cum_ref[:, k0:k0 + tk])
        if masked:
            ok = (k0 + lax.broadcasted_iota(jnp.int32, (1, tk), 1)) <= qpos
            s = jnp.where(ok, s, NEG_BIG)
        m_new = jnp.maximum(m, jnp.max(s, axis=-1, keepdims=True))
        alpha = jnp.exp(m - m_new)
        p = jnp.exp(s - m_new)
        if masked:
            p = jnp.where(ok, p, 0.0)
        l = alpha * l + jnp.sum(p, axis=-1, keepdims=True)
        acc = alpha * acc + jnp.dot(p.astype(BF16), v, preferred_element_type=F32)
        return m_new, l, acc

    init = (jnp.full((tq, 1), NEG_BIG, F32), jnp.zeros((tq, 1), F32), jnp.zeros((tq, HEAD_DIM), F32))
    n_full = t0 // tk
    carry = init
    for kt in range(n_full):
        carry = step(kt, carry, False)
    for d in range(tq // tk):
        carry = step(n_full + d, carry, True)
    _, l, acc = carry
    o_ref[...] = (acc / jnp.maximum(l, 1e-30) * zs_ref[...].astype(F32)).astype(o_ref.dtype)


def _fox_prompt_attn(q_bf, kv_bf, cum, zs, tq=512, tk=512):
    B, T, W = q_bf.shape
    H = N_HEADS
    tq, tk = min(tq, T), min(tk, T)
    assert T % tq == 0 and tq % tk == 0
    outs = []
    for qi in range(T // tq):
        n_keys = (qi + 1) * tq
        outs.append(pl.pallas_call(
            functools.partial(_fox_prompt_kernel, tq, tk, qi),
            grid=(B, H),
            in_specs=[
                pl.BlockSpec((None, tq, HEAD_DIM), lambda b, h, qi=qi: (b, qi, h)),
                pl.BlockSpec((None, n_keys, HEAD_DIM), lambda b, h: (b, 0, h)),
                pl.BlockSpec((None, n_keys, HEAD_DIM), lambda b, h: (b, 0, H + h)),
                pl.BlockSpec((None, None, 1, n_keys), lambda b, h: (b, h, 0, 0)),
                pl.BlockSpec((None, tq, HEAD_DIM), lambda b, h, qi=qi: (b, qi, h)),
            ],
            out_specs=pl.BlockSpec((None, tq, HEAD_DIM), lambda b, h: (b, 0, h)),
            out_shape=jax.ShapeDtypeStruct((B, tq, W), BF16),
            compiler_params=_cparams(("parallel", "parallel")),
            name="fox_prompt_attn",
        )(q_bf, kv_bf, kv_bf, cum, zs))
    return jnp.concatenate(outs, axis=1)


def _fox_sample_kernel(n_pg, page, T, pt_ref, q_ref, *refs):
    blocks = refs[:4 * n_pg]
    cum_ref, kvn_ref, lfn_ref, zs_ref, o_ref, m_ref, l_ref, acc_ref = refs[4 * n_pg:]
    c = pl.program_id(1)
    H = N_HEADS
    n_half = H // SUBLANES

    def head_rows(kv, h):
        return jnp.concatenate(
            [_tile_row(blocks[(p * 2 + kv) * n_half + h // SUBLANES], h % SUBLANES) for p in range(n_pg)], axis=0)

    q = q_ref[...].astype(BF16)

    @pl.when(c == 0)
    def _():
        m_ref[...] = jnp.full(m_ref.shape, NEG_BIG, F32)
        l_ref[...] = jnp.zeros(l_ref.shape, F32)
        acc_ref[...] = jnp.zeros(acc_ref.shape, F32)

    def attend(ks, vs, cum_rows, ok):
        n = ks[0].shape[0]
        s = jnp.concatenate(
            [lax.dot_general(q[:, h * HEAD_DIM:(h + 1) * HEAD_DIM], ks[h], NT, preferred_element_type=F32)
             - jnp.broadcast_to(cum_rows[h], (T, n)) for h in range(H)], axis=0)
        if ok is not None:
            ok = jnp.concatenate([ok] * H, axis=0)
            s = jnp.where(ok, s, NEG_BIG)
        m_new = jnp.maximum(m_ref[:, 0:1], jnp.max(s, axis=-1, keepdims=True))
        alpha = jnp.exp(m_ref[:, 0:1] - m_new)
        p = jnp.exp(s - m_new)
        if ok is not None:
            p = jnp.where(ok, p, 0.0)
        l_new = alpha * l_ref[:, 0:1] + jnp.sum(p, axis=-1, keepdims=True)
        pv = jnp.concatenate(
            [jnp.dot(p[h * T:(h + 1) * T].astype(BF16), vs[h], preferred_element_type=F32) for h in range(H)], axis=0)
        acc_ref[...] = alpha * acc_ref[...] + pv
        m_ref[...] = jnp.broadcast_to(m_new, (H * T, LANES))
        l_ref[...] = jnp.broadcast_to(l_new, (H * T, LANES))

    attend([head_rows(0, h).astype(BF16) for h in range(H)], [head_rows(1, h).astype(BF16) for h in range(H)],
           [cum_ref[h:h + 1, :] for h in range(H)], None)

    @pl.when(c == pl.num_programs(1) - 1)
    def _():
        total = cum_ref[:, n_pg * page - 1:n_pg * page]
        pad = jnp.zeros((LANES - T, LANES), F32)
        acc = jnp.zeros((LANES, LANES), F32)
        lower = _lower_ones(LANES)
        for part in _split3(jnp.concatenate([lfn_ref[...], pad], axis=0)):
            acc = acc + jnp.dot(lower, part, preferred_element_type=F32)
        cum_new = acc.T[:H] + total
        kvn = kvn_ref[...]
        zs = zs_ref[...]
        ok_n = lax.broadcasted_iota(jnp.int32, (1, LANES), 1) <= lax.broadcasted_iota(jnp.int32, (T, 1), 0)
        new = lambda j: jnp.concatenate([kvn[:, j * HEAD_DIM:(j + 1) * HEAD_DIM], pad], axis=0).astype(BF16)
        attend([new(h) for h in range(H)], [new(H + h) for h in range(H)],
               [cum_new[h:h + 1, :] for h in range(H)], ok_n)
        o = acc_ref[...] / jnp.maximum(l_ref[:, 0:1], 1e-30)
        o_ref[...] = jnp.concatenate(
            [o[h * T:(h + 1) * T] * zs[:, h * HEAD_DIM:(h + 1) * HEAD_DIM] for h in range(H)], axis=1)


def _fox_sample_attn(q, kv_pool, layer, page_table, cum, kv_new, logf_new, zs, B, T):
    page = kv_pool.shape[2]
    n_pages = page_table.shape[1]
    n_pg = FOX_PAGES
    H = N_HEADS
    W = H * HEAD_DIM
    assert n_pages % n_pg == 0 and T % SUBLANES == 0 and T <= LANES and H % SUBLANES == 0
    n_half = H // SUBLANES
    block_spec = lambda p, kv, half: pl.BlockSpec(
        (None, None, page, None, SUBLANES, HEAD_DIM),
        lambda b, c, pt: (layer, pt[b, c * n_pg + p], 0, kv, half, 0))
    block_specs = [block_spec(p, kv, half) for p in range(n_pg) for kv in range(2) for half in range(n_half)]
    tok = lambda width: pl.BlockSpec((T, width), lambda b, c, pt: (b, 0))
    return pl.pallas_call(
        functools.partial(_fox_sample_kernel, n_pg, page, T),
        grid_spec=pltpu.PrefetchScalarGridSpec(
            num_scalar_prefetch=1,
            grid=(B, n_pages // n_pg),
            in_specs=[tok(W)] + block_specs + [
                pl.BlockSpec((None, H, n_pg * page), lambda b, c, pt: (b, 0, c)),
                tok(2 * W), tok(LANES), tok(W),
            ],
            out_specs=tok(W),
            scratch_shapes=[
                pltpu.VMEM((H * T, LANES), F32), pltpu.VMEM((H * T, LANES), F32), pltpu.VMEM((H * T, HEAD_DIM), F32),
            ],
        ),
        out_shape=jax.ShapeDtypeStruct((B * T, W), F32),
        compiler_params=_cparams(("parallel", "arbitrary")),
        name="fox_sample_attn",
    )(page_table, q, *([kv_pool] * len(block_specs)), cum, kv_new, logf_new, zs)


def _alibi_slopes():
    s = np.exp2(-8.0 * np.arange(1, N_HEADS + 1) / N_HEADS)
    return jnp.asarray(s, dtype=F32).reshape(A_KV_GROUPS, A_REP)


_NSA_PLAN = (
    (0, 4, "normq", 0, (0,)),
    (4, 6, "raw", 0, (1,)),
    (6, 7, "norm", 2, (1, 2)),
    (7, 8, "raw", 0, (1, 2)),
    (8, 9, "norm", 3, (1, 2)),
    (9, 10, "raw", 0, (1, 2)),
    (10, 14, "silu", 0, (3,)),
)


def _nsa_outs(act_dtype):
    return ((0, 4, act_dtype), (4, 6, F32), (6, 4, act_dtype), (10, 4, act_dtype))


def _nsa_layer(xp, xs, norm_g, w_in, q_gain, k_gain, pe, w1, w2, w_out, slopes, past):
    mix = N_HEADS * HEAD_DIM
    kvw = A_KV_GROUPS * HEAD_DIM
    g0 = mix + 6 * kvw
    g1 = g0 + 3 * N_HEADS
    w_main = jnp.concatenate([w_in[:, :g0], w_in[:, g1:]], axis=1).astype(BF16)
    w_small = jnp.pad(w_in[:, g0:g1], ((0, 0), (0, LANES - 3 * N_HEADS))).astype(BF16)
    gains = jnp.concatenate([q_gain[None], k_gain, jnp.zeros((4, HEAD_DIM), F32)], axis=0)
    zero_bias = jnp.zeros((1, LANES), F32)
    w_out_bf = w_out.astype(BF16)
    slope_tab = jnp.broadcast_to(
        jnp.pad(slopes, ((0, 0), (0, 8 - A_REP)))[:, :, None], (A_KV_GROUPS, 8, LANES))
    shp = lambda B: (B, -1, 2, A_KV_GROUPS, HEAD_DIM)

    B, T, D = xp.shape
    M = B * T
    x2d = xp.reshape(M, D)
    q_bf, rows, rows_bf, zs, gates = _proj(x2d, norm_g, gains, zero_bias, w_main, w_small, _NSA_PLAN,
                                           _nsa_outs(BF16), "sigmoid", 512)
    rows3 = rows.reshape(B, T, -1)
    kvc = _compress_prompt(rows3, pe, w1.astype(BF16), w2.astype(BF16), k_gain[0])
    gates_g = gates[:, :3 * N_HEADS].reshape(B, T, 3, A_KV_GROUPS, A_REP)
    gates_g = jnp.transpose(gates_g, (0, 3, 1, 2, 4)).reshape(B, A_KV_GROUPS, T, 3 * A_REP)
    n_cmp = T // CMP_BLOCK
    expand = (jnp.arange(T, dtype=jnp.int32)[None, :] // SEL_BLOCK
              == jnp.arange(n_cmp, dtype=jnp.int32)[:, None]).astype(BF16)
    u = _nsa_prompt_attn(q_bf.reshape(B, T, -1), kvc, rows_bf.reshape(B, T, -1), gates_g,
                         zs.reshape(B, T, -1), slope_tab, expand)
    yp = _outproj(u.reshape(M, -1), w_out_bf, x2d, 512).reshape(B, T, D)
    out_p = (rows3[:, :, :2 * kvw].reshape(shp(B)), rows3[:, :, 2 * kvw:4 * kvw].reshape(shp(B)),
             rows3[:, T - min(WINDOW, T):, 4 * kvw:].reshape(shp(B)))

    cmp_pool, sel_pool, layer, page_table, win_buf = past
    B, T, D = xs.shape
    M = B * T
    x2d = xs.reshape(M, D)
    q, rows, _, zs, gates = _proj(x2d, norm_g, gains, zero_bias, w_main, w_small, _NSA_PLAN,
                                  _nsa_outs(F32), "sigmoid", M)
    kvc = _compress_paged(cmp_pool, layer, page_table, pe, w1, w2, k_gain[0])
    u = _nsa_sample_attn(q, kvc, sel_pool, layer, page_table, rows, win_buf, gates, zs, slope_tab, B, T)
    ys = _outproj(u, w_out_bf, x2d, M).reshape(B, T, D)
    rows3 = rows.reshape(B, T, -1)
    win_rows = rows3[:, :, 4 * kvw:].reshape(shp(B))
    out_s = (rows3[:, :, :2 * kvw].reshape(shp(B)), rows3[:, :, 2 * kvw:4 * kvw].reshape(shp(B)),
             jnp.concatenate([win_buf[:, T:], win_rows], axis=1))
    return yp, ys, out_p, out_s


_FOX_PLAN = (
    (0, 4, "normq", 0, (0,)),
    (4, 8, "norm", 1, (1, 2)),
    (8, 12, "raw", 0, (1, 2)),
    (12, 16, "silu", 0, (3,)),
)


def _fox_outs(act_dtype):
    return ((0, 4, act_dtype), (4, 8, F32), (4, 8, act_dtype), (12, 4, act_dtype))


def _fox_layer(xp, xs, norm_g, w_in, f_bias, q_gain, k_gain, w_out, past):
    mix = N_HEADS * HEAD_DIM
    w_main = jnp.concatenate([w_in[:, :3 * mix], w_in[:, 3 * mix + N_HEADS:]], axis=1).astype(BF16)
    w_small = jnp.pad(w_in[:, 3 * mix:3 * mix + N_HEADS], ((0, 0), (0, LANES - N_HEADS))).astype(BF16)
    gains = jnp.concatenate([q_gain[None], k_gain[None], jnp.zeros((6, HEAD_DIM), F32)], axis=0)
    sbias = jnp.pad(f_bias, (0, LANES - N_HEADS)).reshape(1, LANES)
    w_out_bf = w_out.astype(BF16)

    B, T, D = xp.shape
    M = B * T
    x2d = xp.reshape(M, D)
    q_bf, kv, kv_bf, zs, logf = _proj(x2d, norm_g, gains, sbias, w_main, w_small, _FOX_PLAN, _fox_outs(BF16),
                                      "logsig", 512)
    cum = _cumsum_heads(logf.reshape(B, T, LANES))[:, :N_HEADS].reshape(B, N_HEADS, 1, T)
    u = _fox_prompt_attn(q_bf.reshape(B, T, -1), kv_bf.reshape(B, T, -1), cum, zs.reshape(B, T, -1))
    yp = _outproj(u.reshape(M, -1), w_out_bf, x2d, 512).reshape(B, T, D)
    out_p = (kv.reshape(B, T, 2, N_HEADS, HEAD_DIM), logf[:, :N_HEADS].reshape(B, T, N_HEADS))

    kv_pool, logf_pool, layer, page_table = past
    B, T, D = xs.shape
    M = B * T
    x2d = xs.reshape(M, D)
    q, kv, _, zs, logf = _proj(x2d, norm_g, gains, sbias, w_main, w_small, _FOX_PLAN, _fox_outs(F32),
                               "logsig", M)
    cum = _cumsum_paged(logf_pool, layer, page_table)
    u = _fox_sample_attn(q, kv_pool, layer, page_table, cum, kv, logf, zs, B, T)
    ys = _outproj(u, w_out_bf, x2d, M).reshape(B, T, D)
    out_s = (kv.reshape(B, T, 2, N_HEADS, HEAD_DIM), logf[:, :N_HEADS].reshape(B, T, N_HEADS))
    return yp, ys, out_p, out_s


def kernel(x_prompt, x_sample, cache_a_cmp, cache_a_sel, state_a_win, cache_b_kv, cache_b_logf, page_table,
           a_norm, a_w_in, a_q_gain, a_k_gain, a_phi_pe, a_phi_w1, a_phi_w2, a_w_out,
           b_norm, b_w_in, b_f_bias, b_q_gain, b_k_gain, b_w_out):
    slopes = _alibi_slopes()
    xp, xs = x_prompt, x_sample
    depth = a_norm.shape[0] + b_norm.shape[0]
    a_p, a_s, b_p, b_s = [], [], [], []
    for i in range(depth):
        j = i // 2
        if i % 2 == 0:
            xp, xs, out_p, out_s = _nsa_layer(
                xp, xs, a_norm[j], a_w_in[j], a_q_gain[j], a_k_gain[j], a_phi_pe[j], a_phi_w1[j], a_phi_w2[j],
                a_w_out[j], slopes, (cache_a_cmp, cache_a_sel, j, page_table, state_a_win[j]))
            a_p.append(out_p)
            a_s.append(out_s)
        else:
            xp, xs, out_p, out_s = _fox_layer(
                xp, xs, b_norm[j], b_w_in[j], b_f_bias[j], b_q_gain[j], b_k_gain[j], b_w_out[j],
                (cache_b_kv, cache_b_logf, j, page_table))
            b_p.append(out_p)
            b_s.append(out_s)
    stack = lambda outs, k: jnp.stack([o[k] for o in outs])
    return (xp, xs,
            stack(a_p, 0), stack(a_s, 0), stack(a_p, 1), stack(a_s, 1), stack(a_p, 2), stack(a_s, 2),
            stack(b_p, 0), stack(b_s, 0), stack(b_p, 1), stack(b_s, 1))
```

```python
import functools

import jax
import jax.numpy as jnp
import numpy as np
from jax import lax
from jax.experimental import pallas as pl
from jax.experimental.pallas import tpu as pltpu

HEAD_DIM = 128
N_HEADS = 16
A_KV_GROUPS = 4
A_REP = N_HEADS // A_KV_GROUPS
CMP_BLOCK = 64
SEL_BLOCK = CMP_BLOCK
TOP_N = 16
WINDOW = 512
RMS_EPS = 1e-6
SCALE = HEAD_DIM ** -0.5
FORCED_SCORE = float(A_REP + 1)
NEG_BIG = -1e30

LANES = 128
SUBLANES = 8
PROJ_COLS = 512
PROJ_ROWS = 1024
VMEM_LIMIT = 56 * 1024 * 1024
CMP_PAGES = 16
SEL_PAGES = 8
FOX_PAGES = 4
LOGF_PAGES = 8
ROW_CHUNK = 128

F32 = jnp.float32
BF16 = jnp.bfloat16
NT = (((1,), (1,)), ((), ()))
TN = (((0,), (0,)), ((), ()))


def _cparams(sem):
    return pltpu.CompilerParams(dimension_semantics=sem, vmem_limit_bytes=VMEM_LIMIT)


def _head_rms(a, gain):
    ms = jnp.mean(a * a, axis=-1, keepdims=True)
    return a * lax.rsqrt(ms + RMS_EPS) * gain


def _softmax_rows(s, ok):
    m = jnp.max(s, axis=-1, keepdims=True)
    p = jnp.where(ok, jnp.exp(s - m), 0.0)
    return p / jnp.maximum(jnp.sum(p, axis=-1, keepdims=True), 1e-30)


def _split3(x):
    hi = x.astype(BF16)
    r1 = x - hi.astype(F32)
    mid = r1.astype(BF16)
    lo = (r1 - mid.astype(F32)).astype(BF16)
    return hi, mid, lo


def _lower_ones(n):
    tri = lax.broadcasted_iota(jnp.int32, (n, n), 0) >= lax.broadcasted_iota(jnp.int32, (n, n), 1)
    return jnp.where(tri, 1.0, 0.0).astype(BF16)


def _tile_row(block_ref, j):
    n, _, d = block_ref.shape
    return block_ref.reshape(n * SUBLANES, d)[pl.ds(j, n, stride=SUBLANES), :]


def _rank_counts(score, blk, n_blk):
    rank = jnp.zeros(score.shape, F32)
    for i in range(n_blk):
        col = score[:, i:i + 1]
        rank = rank + jnp.where(blk > i, jnp.where(col >= score, 1.0, 0.0), jnp.where(col > score, 1.0, 0.0))
    return rank


def _proj_kernel(plan, small_kind, x_ref, ng_ref, gains_ref, sbias_ref, w_ref, ws_ref, *refs):
    n_out = len(refs) - 1
    outs, xn_ref = refs[:n_out], refs[n_out]
    j = pl.program_id(1)

    @pl.when(j == 0)
    def _():
        x = x_ref[...]
        ms = jnp.mean(x * x, axis=-1, keepdims=True)
        xn_ref[...] = (x * lax.rsqrt(ms + RMS_EPS) * ng_ref[...]).astype(BF16)
        small = jnp.dot(xn_ref[...], ws_ref[...], preferred_element_type=F32)
        if small_kind == "sigmoid":
            small = jax.nn.sigmoid(small)
        else:
            small = small + sbias_ref[...]
            small = jnp.minimum(small, 0.0) - jnp.log1p(jnp.exp(-jnp.abs(small)))
        outs[n_out - 1][...] = small

    for (j0, j1, kind, gain_row, targets) in plan:
        @pl.when((j >= j0) & (j < j1))
        def _(kind=kind, gain_row=gain_row, targets=targets):
            acc = jnp.dot(xn_ref[...], w_ref[...], preferred_element_type=F32)
            if kind in ("norm", "normq"):
                g = gains_ref[gain_row:gain_row + 1, :]
                if kind == "normq":
                    g = g * SCALE
                acc = jnp.concatenate(
                    [_head_rms(acc[:, h * LANES:(h + 1) * LANES], g) for h in range(PROJ_COLS // LANES)], axis=1)
            elif kind == "silu":
                acc = acc * jax.nn.sigmoid(acc)
            for t in targets:
                outs[t][...] = acc.astype(outs[t].dtype)


def _proj(x2d, norm_gain, gains, sbias, w_main, w_small, plan, out_defs, small_kind, bm):
    M, D = x2d.shape
    nblk = w_main.shape[1] // PROJ_COLS
    grid = (M // bm, nblk)

    def out_map(j0, n):
        return lambda i, j: (i, jnp.clip(j - j0, 0, n - 1))

    out_shapes = [jax.ShapeDtypeStruct((M, n * PROJ_COLS), dt) for (_, n, dt) in out_defs]
    out_specs = [pl.BlockSpec((bm, PROJ_COLS), out_map(j0, n)) for (j0, n, _) in out_defs]
    out_shapes.append(jax.ShapeDtypeStruct((M, LANES), F32))
    out_specs.append(pl.BlockSpec((bm, LANES), lambda i, j: (i, 0)))
    return pl.pallas_call(
        functools.partial(_proj_kernel, plan, small_kind),
        grid=grid,
        in_specs=[
            pl.BlockSpec((bm, D), lambda i, j: (i, 0)),
            pl.BlockSpec((1, D), lambda i, j: (0, 0)),
            pl.BlockSpec((8, LANES), lambda i, j: (0, 0)),
            pl.BlockSpec((1, LANES), lambda i, j: (0, 0)),
            pl.BlockSpec((D, PROJ_COLS), lambda i, j: (0, j)),
            pl.BlockSpec((D, LANES), lambda i, j: (0, 0)),
        ],
        out_specs=out_specs,
        out_shape=out_shapes,
        scratch_shapes=[pltpu.VMEM((bm, D), BF16)],
        compiler_params=_cparams(("parallel", "arbitrary")),
        name="proj",
    )(x2d, norm_gain.reshape(1, D), gains, sbias, w_main, w_small)


def _outproj_kernel(u_ref, w_ref, x_ref, o_ref):
    o_ref[...] = x_ref[...] + jnp.dot(u_ref[...].astype(BF16), w_ref[...], preferred_element_type=F32)


def _outproj(u2d, w_bf, x2d, bm):
    M, K = u2d.shape
    N = w_bf.shape[1]
    return pl.pallas_call(
        _outproj_kernel,
        grid=(M // bm,),
        in_specs=[
            pl.BlockSpec((bm, K), lambda i: (i, 0)),
            pl.BlockSpec((K, N), lambda i: (0, 0)),
            pl.BlockSpec((bm, N), lambda i: (i, 0)),
        ],
        out_specs=pl.BlockSpec((bm, N), lambda i: (i, 0)),
        out_shape=jax.ShapeDtypeStruct((M, N), F32),
        compiler_params=_cparams(("parallel",)),
        name="outproj",
    )(u2d, w_bf, x2d)


def _compress_kernel(n_blk, x0_ref, x1_ref, x2_ref, x3_ref, pe_ref, w1_ref, w2_ref, kg_ref, o_ref):
    kv = pl.program_id(1)

    def body(i, acc):
        pe = pe_ref[pl.ds(i, 1), :]
        a = jnp.concatenate(
            [x_ref[pl.ds(i, n_blk, stride=CMP_BLOCK), :] + pe for x_ref in (x0_ref, x1_ref, x2_ref, x3_ref)],
            axis=0)
        return acc + jnp.dot(a.astype(BF16), w1_ref[i], preferred_element_type=F32)

    acc = lax.fori_loop(0, CMP_BLOCK, body, jnp.zeros((A_KV_GROUPS * n_blk, HEAD_DIM), F32))
    h = jax.nn.gelu(acc)
    out = jnp.dot(h.astype(BF16), w2_ref[...], preferred_element_type=F32)
    out = jnp.where(kv == 0, _head_rms(out, kg_ref[...]), out)
    for g in range(A_KV_GROUPS):
        o_ref[g] = out[g * n_blk:(g + 1) * n_blk].astype(o_ref.dtype)


def _compress_prompt(rows, pe, w1_bf, w2_bf, k_gain_cmp):
    B, T = rows.shape[0], rows.shape[1]
    n_blk = T // CMP_BLOCK
    G = A_KV_GROUPS
    grp = lambda g: pl.BlockSpec((None, T, HEAD_DIM), lambda b, kv, g=g: (b, 0, kv * G + g))
    return pl.pallas_call(
        functools.partial(_compress_kernel, n_blk),
        grid=(B, 2),
        in_specs=[
            grp(0), grp(1), grp(2), grp(3),
            pl.BlockSpec((None, CMP_BLOCK, HEAD_DIM), lambda b, kv: (kv, 0, 0)),
            pl.BlockSpec((None, CMP_BLOCK, HEAD_DIM, HEAD_DIM), lambda b, kv: (kv, 0, 0, 0)),
            pl.BlockSpec((None, HEAD_DIM, HEAD_DIM), lambda b, kv: (kv, 0, 0)),
            pl.BlockSpec((1, HEAD_DIM), lambda b, kv: (0, 0)),
        ],
        out_specs=pl.BlockSpec((None, None, A_KV_GROUPS, n_blk, HEAD_DIM), lambda b, kv: (b, kv, 0, 0, 0)),
        out_shape=jax.ShapeDtypeStruct((B, 2, A_KV_GROUPS, n_blk, HEAD_DIM), BF16),
        compiler_params=_cparams(("parallel", "arbitrary")),
        name="compress",
    )(rows, rows, rows, rows, pe, w1_bf, w2_bf, k_gain_cmp.reshape(1, HEAD_DIM))


def _compress_paged_kernel(n_pg, bpp, pt_ref, *refs):
    pages = refs[:n_pg]
    pe_ref, w1_ref, w2_ref, kg_ref, o_ref = refs[n_pg:]
    n_rows = n_pg * bpp * SUBLANES
    slabs = []
    for i in range(CMP_BLOCK):
        pe = pe_ref[i]
        a = jnp.concatenate([pg[h * CMP_BLOCK + i] + pe for pg in pages for h in range(bpp)], axis=0)
        slabs.append(a.astype(BF16))
    acc = jnp.dot(jnp.concatenate(slabs, axis=1), w1_ref[...], preferred_element_type=F32)
    h = jax.nn.gelu(acc)
    out_k = jnp.dot(h[:, :HEAD_DIM].astype(BF16), w2_ref[0], preferred_element_type=F32)
    out_v = jnp.dot(h[:, HEAD_DIM:].astype(BF16), w2_ref[1], preferred_element_type=F32)
    out_k = _head_rms(out_k, kg_ref[...])
    is_k = (lax.broadcasted_iota(jnp.int32, (n_rows, 1), 0) & (SUBLANES - 1)) < A_KV_GROUPS
    o_ref[...] = jnp.where(is_k, out_k, out_v)


def _compress_paged(pool, layer, page_table, pe, w1, w2, k_gain_cmp):
    page = pool.shape[2]
    bpp = page // CMP_BLOCK
    B, n_pages = page_table.shape
    n_pg = CMP_PAGES
    assert page % CMP_BLOCK == 0 and n_pages % n_pg == 0 and 2 * A_KV_GROUPS == SUBLANES
    pool5 = pool.reshape(pool.shape[0], pool.shape[1], page, SUBLANES, HEAD_DIM)
    pe8 = jnp.repeat(jnp.transpose(pe, (1, 0, 2)), A_KV_GROUPS, axis=1)
    w1cat = jnp.concatenate([w1[0], w1[1]], axis=-1).astype(BF16)
    w1cat = w1cat.reshape(CMP_BLOCK * HEAD_DIM, 2 * HEAD_DIM)
    rows_per_step = n_pg * bpp * SUBLANES
    page_spec = lambda p: pl.BlockSpec(
        (None, None, page, SUBLANES, HEAD_DIM), lambda b, c, pt, p=p: (layer, pt[b, c * n_pg + p], 0, 0, 0))
    return pl.pallas_call(
        functools.partial(_compress_paged_kernel, n_pg, bpp),
        grid_spec=pltpu.PrefetchScalarGridSpec(
            num_scalar_prefetch=1,
            grid=(B, n_pages // n_pg),
            in_specs=[page_spec(p) for p in range(n_pg)] + [
                pl.BlockSpec((CMP_BLOCK, SUBLANES, HEAD_DIM), lambda b, c, pt: (0, 0, 0)),
                pl.BlockSpec((CMP_BLOCK * HEAD_DIM, 2 * HEAD_DIM), lambda b, c, pt: (0, 0)),
                pl.BlockSpec((2, HEAD_DIM, HEAD_DIM), lambda b, c, pt: (0, 0, 0)),
                pl.BlockSpec((1, HEAD_DIM), lambda b, c, pt: (0, 0)),
            ],
            out_specs=pl.BlockSpec((None, rows_per_step, HEAD_DIM), lambda b, c, pt: (b, c, 0)),
        ),
        out_shape=jax.ShapeDtypeStruct((B, n_pages * bpp * SUBLANES, HEAD_DIM), F32),
        compiler_params=_cparams(("parallel", "arbitrary")),
        name="compress_paged",
    )(page_table, *([pool5] * n_pg), pe8, w1cat, w2.astype(BF16), k_gain_cmp.reshape(1, HEAD_DIM))


def _select_blocks_t(imp_t, cur, n_blk):
    blk = lax.broadcasted_iota(jnp.int32, (n_blk, 1), 0)
    valid = blk <= cur
    forced = (blk == 0) | (blk == cur) | (blk == cur - 1)
    score = jnp.where(valid, jnp.where(forced, FORCED_SCORE, imp_t), -jnp.inf)
    sub = lax.broadcasted_iota(jnp.int32, (SUBLANES, 1), 0)
    ranks = []
    for b0 in range(0, n_blk, SUBLANES):
        slab = score[b0:b0 + SUBLANES]
        rank = jnp.zeros(slab.shape, F32)
        for i in range(n_blk):
            row = score[i:i + 1]
            if i < b0:
                rank = rank + jnp.where(row >= slab, 1.0, 0.0)
            elif i >= b0 + SUBLANES:
                rank = rank + jnp.where(row > slab, 1.0, 0.0)
            else:
                rank = rank + jnp.where(sub > i - b0, jnp.where(row >= slab, 1.0, 0.0),
                                        jnp.where(row > slab, 1.0, 0.0))
        ranks.append(rank)
    rank = jnp.concatenate(ranks, axis=0)
    return jnp.where(valid & (rank < float(min(TOP_N, n_blk))), 1.0, 0.0)


def _nsa_prompt_kernel(tq, tk, T, n_full, q_ref, kc_ref, vc_ref, ks_ref, vs_ref, kw_ref, vw_ref, gt_ref, zs_ref,
                       slope_ref, exp_ref, *rest):
    o_ref = rest[-1]
    qi = n_full * (tk // tq) + pl.program_id(2)
    t0 = qi * tq
    n_cmp = T // CMP_BLOCK
    R = A_REP
    q = q_ref[...]
    q4 = jnp.concatenate([q[:, r * HEAD_DIM:(r + 1) * HEAD_DIM] for r in range(R)], axis=0)
    qpos = t0 + lax.broadcasted_iota(jnp.int32, (tq, 1), 0)
    qpos_row = t0 + lax.broadcasted_iota(jnp.int32, (1, tq), 1)
    slopes = [slope_ref[r:r + 1, 0:1] for r in range(R)]

    sct = lax.dot_general(kc_ref[...], q4, NT, preferred_element_type=F32)
    c_end = (lax.broadcasted_iota(jnp.int32, (n_cmp, 1), 0) + 1) * CMP_BLOCK - 1
    ok_c = qpos_row >= c_end
    cposf = (c_end - t0).astype(F32)
    p_parts = []
    for r in range(R):
        s_r = jnp.where(ok_c, sct[:, r * tq:(r + 1) * tq] + slopes[r] * cposf, NEG_BIG)
        p_r = jnp.where(ok_c, jnp.exp(s_r - jnp.max(s_r, axis=0, keepdims=True)), 0.0)
        p_parts.append(p_r / jnp.maximum(jnp.sum(p_r, axis=0, keepdims=True), 1e-30))
    p_ct = jnp.concatenate(p_parts, axis=1).astype(BF16)
    o_cmp = lax.dot_general(p_ct, vc_ref[...], TN, preferred_element_type=F32)
    imp_t = p_parts[0]
    for r in range(1, R):
        imp_t = imp_t + p_parts[r]
    sel_t = _select_blocks_t(imp_t, qpos_row // SEL_BLOCK, n_cmp).astype(BF16)

    def sel_tile(kt, carry, diagonal):
        m, l, acc = carry
        k0 = kt * tk
        s = lax.dot_general(q4, ks_ref[k0:k0 + tk, :], NT, preferred_element_type=F32)
        v = vs_ref[k0:k0 + tk, :]
        kpos = k0 + lax.broadcasted_iota(jnp.int32, (1, tk), 1)
        picked = lax.dot_general(sel_t, exp_ref[:, k0:k0 + tk], TN, preferred_element_type=F32)
        mask = (picked - 1.0) * (-NEG_BIG)
        if diagonal:
            mask = jnp.where(kpos <= qpos, mask, NEG_BIG)
        kposf = (kpos - t0).astype(F32)
        s = jnp.concatenate(
            [s[r * tq:(r + 1) * tq] + (mask + slopes[r] * kposf) for r in range(R)], axis=0)
        m_new = jnp.maximum(m, jnp.max(s, axis=-1, keepdims=True))
        alpha = jnp.exp(m - m_new)
        p = jnp.exp(s - m_new)
        l = alpha * l + jnp.sum(p, axis=-1, keepdims=True)
        acc = alpha * acc + jnp.dot(p.astype(BF16), v, preferred_element_type=F32)
        return m_new, l, acc

    carry = (jnp.full((R * tq, 1), NEG_BIG, F32), jnp.zeros((R * tq, 1), F32), jnp.zeros((R * tq, HEAD_DIM), F32))
    for kt in range(n_full):
        carry = sel_tile(kt, carry, False)
    _, l_s, acc_s = sel_tile(n_full, carry, True)
    o_sel = acc_s / l_s

    span = WINDOW + tq
    w0 = pl.multiple_of(jnp.maximum(t0 - WINDOW, 0), tq)
    kw = kw_ref[pl.ds(w0, span), :]
    vw = vw_ref[pl.ds(w0, span), :]
    sw = lax.dot_general(q4, kw, NT, preferred_element_type=F32)
    wpos = w0 + lax.broadcasted_iota(jnp.int32, (1, span), 1)
    d_w = qpos - wpos
    mask_w = jnp.where((d_w >= 0) & (d_w < WINDOW), 0.0, NEG_BIG)
    wposf = (wpos - t0).astype(F32)
    sw = jnp.concatenate(
        [sw[r * tq:(r + 1) * tq] + (mask_w + slopes[r] * wposf) for r in range(R)], axis=0)
    pw = jnp.exp(sw - jnp.max(sw, axis=-1, keepdims=True))
    o_win = (jnp.dot(pw.astype(BF16), vw, preferred_element_type=F32)
             / jnp.sum(pw, axis=-1, keepdims=True))

    gt = gt_ref[...]
    zs = zs_ref[...]
    outs = []
    for r in range(R):
        rows = slice(r * tq, (r + 1) * tq)
        o_r = (gt[:, r:r + 1] * o_cmp[rows] + gt[:, R + r:R + r + 1] * o_sel[rows]
               + gt[:, 2 * R + r:2 * R + r + 1] * o_win[rows])
        outs.append(o_r * zs[:, r * HEAD_DIM:(r + 1) * HEAD_DIM].astype(F32))
    o_ref[...] = jnp.concatenate(outs, axis=1).astype(o_ref.dtype)


def _nsa_prompt_attn(q_bf, kvc, rows_bf, gates_g, zs, slope_tab, expand, tq=256, tk=512):
    B, T, W = q_bf.shape
    G = A_KV_GROUPS
    gw = A_REP * HEAD_DIM
    n_cmp = T // CMP_BLOCK
    tk = min(tk, T)
    assert T % tk == 0 and tk % tq == 0 and T >= WINDOW + tq and tk >= WINDOW
    per_call = tk // tq
    out = None
    for n_full in range(T // tk):
        n_keys = (n_full + 1) * tk
        q0 = n_full * per_call
        n_win = max(n_keys, WINDOW + tq)
        prefix = lambda col, rows=n_keys: pl.BlockSpec(
            (None, rows, HEAD_DIM), lambda b, g, i, col=col: (b, 0, col * G + g))
        tile = pl.BlockSpec((None, tq, gw), lambda b, g, i, q0=q0: (b, q0 + i, g))
        in_specs = [
            tile,
            pl.BlockSpec((None, None, None, n_cmp, HEAD_DIM), lambda b, g, i: (b, 0, g, 0, 0)),
            pl.BlockSpec((None, None, None, n_cmp, HEAD_DIM), lambda b, g, i: (b, 1, g, 0, 0)),
            prefix(0), prefix(1), prefix(2, n_win), prefix(3, n_win),
            pl.BlockSpec((None, None, tq, 3 * A_REP), lambda b, g, i, q0=q0: (b, g, q0 + i, 0)),
            tile,
            pl.BlockSpec((None, 8, LANES), lambda b, g, i: (g, 0, 0)),
            pl.BlockSpec((n_cmp, n_keys), lambda b, g, i: (0, 0)),
        ]
        args = [q_bf, kvc, kvc, rows_bf, rows_bf, rows_bf, rows_bf, gates_g, zs, slope_tab, expand]
        aliases = {}
        if out is not None:
            in_specs.append(pl.BlockSpec(memory_space=pl.ANY))
            args.append(out)
            aliases = {len(args) - 1: 0}
        out = pl.pallas_call(
            functools.partial(_nsa_prompt_kernel, tq, tk, T, n_full),
            grid=(B, G, per_call),
            in_specs=in_specs,
            out_specs=tile,
            out_shape=jax.ShapeDtypeStruct((B, T, W), BF16),
            input_output_aliases=aliases,
            compiler_params=_cparams(("parallel", "parallel", "arbitrary")),
            name="nsa_prompt_attn",
        )(*args)
    return out


def _nsa_sample_kernel(n_pg, page, P_len, T, wb, pt_ref, q_ref, kvc_ref, *refs):
    pages = refs[:n_pg]
    rows_ref, win_ref, gt_ref, zs_ref, slope_ref, o_ref, m_ref, l_ref, acc_ref, sel_ref, ocmp_ref = refs[n_pg:]
    c = pl.program_id(1)
    G, R = A_KV_GROUPS, A_REP
    n_blk = P_len // SEL_BLOCK
    q = q_ref[...]
    tpos = lax.broadcasted_iota(jnp.int32, (T, 1), 0)

    def q4(g):
        return jnp.concatenate(
            [q[:, (g * R + r) * HEAD_DIM:(g * R + r + 1) * HEAD_DIM] for r in range(R)], axis=0).astype(BF16)

    def slope(g, r):
        return slope_ref[g, r:r + 1, 0:1]

    def flash_update(g, k, v, ok, kposf):
        s = lax.dot_general(q4(g), k, NT, preferred_element_type=F32)
        s = jnp.concatenate(
            [jnp.where(ok, s[r * T:(r + 1) * T] + slope(g, r) * kposf, NEG_BIG) for r in range(R)], axis=0)
        ok4 = jnp.concatenate([ok] * R, axis=0)
        m = m_ref[g][:, 0:1]
        m_new = jnp.maximum(m, jnp.max(s, axis=-1, keepdims=True))
        alpha = jnp.exp(m - m_new)
        p = jnp.where(ok4, jnp.exp(s - m_new), 0.0)
        l_new = alpha * l_ref[g][:, 0:1] + jnp.sum(p, axis=-1, keepdims=True)
        acc_ref[g] = alpha * acc_ref[g] + jnp.dot(p.astype(BF16), v, preferred_element_type=F32)
        m_ref[g] = jnp.broadcast_to(m_new, (R * T, LANES))
        l_ref[g] = jnp.broadcast_to(l_new, (R * T, LANES))

    @pl.when(c == 0)
    def _():
        c_end = (lax.broadcasted_iota(jnp.int32, (1, n_blk), 1) + 1) * CMP_BLOCK - 1
        cposf = (c_end - P_len).astype(F32)
        ok_c = (P_len + tpos) >= c_end
        imps = []
        for g in range(G):
            kc = kvc_ref[pl.ds(g, n_blk, stride=SUBLANES), :].astype(BF16)
            vc = kvc_ref[pl.ds(G + g, n_blk, stride=SUBLANES), :].astype(BF16)
            sc = lax.dot_general(q4(g), kc, NT, preferred_element_type=F32)
            parts = []
            for r in range(R):
                s_r = sc[r * T:(r + 1) * T] + slope(g, r) * cposf
                parts.append(_softmax_rows(jnp.where(ok_c, s_r, NEG_BIG), ok_c))
            ocmp_ref[g] = jnp.dot(jnp.concatenate(parts, axis=0).astype(BF16), vc, preferred_element_type=F32)
            imps.append(parts[0] + parts[1] + parts[2] + parts[3])
        imp = jnp.concatenate(imps, axis=0)
        blk = lax.broadcasted_iota(jnp.int32, (1, n_blk), 1)
        forced = (blk == 0) | (blk == n_blk - 1)
        score = jnp.where(forced, FORCED_SCORE, imp)
        rank = _rank_counts(score, blk, n_blk) + jnp.where(FORCED_SCORE > score, 1.0, 0.0)
        sel_ref[...] = jnp.where(rank < float(TOP_N), 1.0, 0.0)
        m_ref[...] = jnp.full(m_ref.shape, NEG_BIG, F32)
        l_ref[...] = jnp.zeros(l_ref.shape, F32)
        acc_ref[...] = jnp.zeros(acc_ref.shape, F32)

    n_keys = n_pg * page
    kpos = c * n_keys + lax.broadcasted_iota(jnp.int32, (1, n_keys), 1)
    kposf = (kpos - P_len).astype(F32)
    rowblk = lax.broadcasted_iota(jnp.int32, (n_blk, 1), 0)
    expand = jnp.where(rowblk == kpos // SEL_BLOCK, 1.0, 0.0).astype(BF16)
    picked = jnp.dot(sel_ref[...].astype(BF16), expand, preferred_element_type=F32)
    for g in range(G):
        k = jnp.concatenate([pg[pl.ds(g, page, stride=SUBLANES), :] for pg in pages], axis=0).astype(BF16)
        v = jnp.concatenate([pg[pl.ds(G + g, page, stride=SUBLANES), :] for pg in pages], axis=0).astype(BF16)
        flash_update(g, k, v, picked[g * T:(g + 1) * T] > 0.5, kposf)

    @pl.when(c == pl.num_programs(1) - 1)
    def _():
        kvw = G * HEAD_DIM
        rows = rows_ref[...]
        gt = gt_ref[...]
        zs = zs_ref[...]
        pad = jnp.zeros((LANES - T, HEAD_DIM), F32)
        lane = lax.broadcasted_iota(jnp.int32, (1, LANES), 1)
        widx = lax.broadcasted_iota(jnp.int32, (1, wb + LANES), 1)
        ok_w = (widx > tpos + (wb - WINDOW)) & (widx <= tpos + wb)
        wposf = (widx - wb).astype(F32)
        outs = []
        for g in range(G):
            new = lambda sec: jnp.concatenate(
                [rows[:, sec * kvw + g * HEAD_DIM:sec * kvw + (g + 1) * HEAD_DIM], pad], axis=0)
            flash_update(g, new(2).astype(BF16), new(3).astype(BF16), lane <= tpos, lane.astype(F32))
            o_sel = acc_ref[g] / jnp.maximum(l_ref[g][:, 0:1], 1e-30)
            kw = jnp.concatenate([win_ref[pl.ds(g, wb, stride=SUBLANES), :], new(4)], axis=0).astype(BF16)
            vw = jnp.concatenate([win_ref[pl.ds(G + g, wb, stride=SUBLANES), :], new(5)], axis=0).astype(BF16)
            sw = lax.dot_general(q4(g), kw, NT, preferred_element_type=F32)
            parts = []
            for r in range(R):
                s_r = sw[r * T:(r + 1) * T] + slope(g, r) * wposf
                parts.append(_softmax_rows(jnp.where(ok_w, s_r, NEG_BIG), ok_w))
            o_win = jnp.dot(jnp.concatenate(parts, axis=0).astype(BF16), vw, preferred_element_type=F32)
            o_cmp = ocmp_ref[g]
            for r in range(R):
                h = g * R + r
                rs = slice(r * T, (r + 1) * T)
                o_r = (gt[:, h:h + 1] * o_cmp[rs] + gt[:, N_HEADS + h:N_HEADS + h + 1] * o_sel[rs]
                       + gt[:, 2 * N_HEADS + h:2 * N_HEADS + h + 1] * o_win[rs])
                outs.append(o_r * zs[:, h * HEAD_DIM:(h + 1) * HEAD_DIM])
        o_ref[...] = jnp.concatenate(outs, axis=1)


def _nsa_sample_attn(q, kvc, sel_pool, layer, page_table, rows, win_buf, gates, zs, slope_tab, B, T):
    page = sel_pool.shape[2]
    n_pages = page_table.shape[1]
    P_len = n_pages * page
    wb = win_buf.shape[1]
    n_pg = SEL_PAGES
    n_blk = P_len // SEL_BLOCK
    G, R = A_KV_GROUPS, A_REP
    W = N_HEADS * HEAD_DIM
    assert P_len % SEL_BLOCK == 0 and T <= SEL_BLOCK and T % SUBLANES == 0 and n_blk + 1 > TOP_N
    assert n_pages % n_pg == 0 and WINDOW <= wb <= P_len and T <= LANES
    pool4 = sel_pool.reshape(sel_pool.shape[0], sel_pool.shape[1], page * SUBLANES, HEAD_DIM)
    win3 = win_buf.reshape(B, wb * SUBLANES, HEAD_DIM)
    page_spec = lambda p: pl.BlockSpec(
        (None, None, page * SUBLANES, HEAD_DIM), lambda b, c, pt, p=p: (layer, pt[b, c * n_pg + p], 0, 0))
    tok = lambda width: pl.BlockSpec((T, width), lambda b, c, pt: (b, 0))
    return pl.pallas_call(
        functools.partial(_nsa_sample_kernel, n_pg, page, P_len, T, wb),
        grid_spec=pltpu.PrefetchScalarGridSpec(
            num_scalar_prefetch=1,
            grid=(B, n_pages // n_pg),
            in_specs=[tok(W), pl.BlockSpec((None, n_blk * SUBLANES, HEAD_DIM), lambda b, c, pt: (b, 0, 0))]
            + [page_spec(p) for p in range(n_pg)] + [
                tok(rows.shape[1]),
                pl.BlockSpec((None, wb * SUBLANES, HEAD_DIM), lambda b, c, pt: (b, 0, 0)),
                tok(LANES), tok(W),
                pl.BlockSpec((G, 8, LANES), lambda b, c, pt: (0, 0, 0)),
            ],
            out_specs=tok(W),
            scratch_shapes=[
                pltpu.VMEM((G, R * T, LANES), F32), pltpu.VMEM((G, R * T, LANES), F32),
                pltpu.VMEM((G, R * T, HEAD_DIM), F32), pltpu.VMEM((G * T, n_blk), F32),
                pltpu.VMEM((G, R * T, HEAD_DIM), F32),
            ],
        ),
        out_shape=jax.ShapeDtypeStruct((B * T, W), F32),
        compiler_params=_cparams(("parallel", "arbitrary")),
        name="nsa_sample_attn",
    )(page_table, q, kvc, *([pool4] * n_pg), rows, win3, gates, zs, slope_tab)


def _cumsum_kernel(tb, lf_ref, o_ref, carry_ref):
    @pl.when(pl.program_id(1) == 0)
    def _():
        carry_ref[...] = jnp.zeros_like(carry_ref)

    lower = _lower_ones(tb)
    acc = jnp.zeros((tb, LANES), F32)
    for part in _split3(lf_ref[...]):
        acc = acc + jnp.dot(lower, part, preferred_element_type=F32)
    acc = acc.T + carry_ref[...]
    o_ref[...] = acc
    carry_ref[...] = acc[:, tb - 1:tb]


def _cumsum_heads(logf, tb=256):
    B, T, _ = logf.shape
    tb = min(tb, T)
    return pl.pallas_call(
        functools.partial(_cumsum_kernel, tb),
        grid=(B, T // tb),
        in_specs=[pl.BlockSpec((None, tb, LANES), lambda b, i: (b, i, 0))],
        out_specs=pl.BlockSpec((None, LANES, tb), lambda b, i: (b, 0, i)),
        out_shape=jax.ShapeDtypeStruct((B, LANES, T), F32),
        scratch_shapes=[pltpu.VMEM((LANES, 1), F32)],
        compiler_params=_cparams(("parallel", "arbitrary")),
        name="fox_cumsum",
    )(logf)


def _cumsum_paged_kernel(n_pg, page, H, pt_ref, *refs):
    pages = refs[:n_pg]
    o_ref, carry_ref = refs[n_pg:]

    @pl.when(pl.program_id(1) == 0)
    def _():
        carry_ref[...] = jnp.zeros_like(carry_ref)

    lower = _lower_ones(page)
    widen = (lax.broadcasted_iota(jnp.int32, (H, LANES), 0) == lax.broadcasted_iota(jnp.int32, (H, LANES), 1))
    widen = jnp.where(widen, 1.0, 0.0).astype(BF16)
    carry = carry_ref[...]
    for p, pg in enumerate(pages):
        acc = jnp.zeros((page, LANES), F32)
        for part in _split3(pg[...]):
            wide = jnp.dot(part, widen, preferred_element_type=F32).astype(BF16)
            acc = acc + jnp.dot(lower, wide, preferred_element_type=F32)
        cum = acc.T + carry
        o_ref[:, p * page:(p + 1) * page] = cum
        carry = cum[:, page - 1:page]
    carry_ref[...] = carry


def _cumsum_paged(pool, layer, page_table):
    page, H = pool.shape[2], pool.shape[3]
    B, n_pages = page_table.shape
    n_pg = LOGF_PAGES
    assert n_pages % n_pg == 0 and page % LANES == 0
    page_spec = lambda p: pl.BlockSpec(
        (None, None, page, H), lambda b, c, pt, p=p: (layer, pt[b, c * n_pg + p], 0, 0))
    return pl.pallas_call(
        functools.partial(_cumsum_paged_kernel, n_pg, page, H),
        grid_spec=pltpu.PrefetchScalarGridSpec(
            num_scalar_prefetch=1,
            grid=(B, n_pages // n_pg),
            in_specs=[page_spec(p) for p in range(n_pg)],
            out_specs=pl.BlockSpec((None, LANES, n_pg * page), lambda b, c, pt: (b, 0, c)),
            scratch_shapes=[pltpu.VMEM((LANES, 1), F32)],
        ),
        out_shape=jax.ShapeDtypeStruct((B, LANES, n_pages * page), F32),
        compiler_params=_cparams(("parallel", "arbitrary")),
        name="fox_cumsum_paged",
    )(page_table, *([pool] * n_pg))


def _fox_prompt_kernel(tq, tk, qi, q_ref, k_ref, v_ref, cum_ref, zs_ref, *rest):
    o_ref = rest[-1]
    t0 = qi * tq
    q = q_ref[...]
    qpos = t0 + lax.broadcasted_iota(jnp.int32, (tq, 1), 0)

    def step(kt, carry, masked):
        m, l, acc = carry
        k0 = kt * tk
        v = v_ref[k0:k0 + tk, :]
        s = (lax.dot_general(q, k_ref[k0:k0 + tk, :], NT, preferred_element_type=F32)
             - cum_ref[:, k0:k0 + tk])
        if masked:
            ok = (k0 + lax.broadcasted_iota(jnp.int32, (1, tk), 1)) <= qpos
            s = jnp.where(ok, s, NEG_BIG)
        m_new = jnp.maximum(m, jnp.max(s, axis=-1, keepdims=True))
        alpha = jnp.exp(m - m_new)
        p = jnp.exp(s - m_new)
        if masked:
            p = jnp.where(ok, p, 0.0)
        l = alpha * l + jnp.sum(p, axis=-1, keepdims=True)
        acc = alpha * acc + jnp.dot(p.astype(BF16), v, preferred_element_type=F32)
        return m_new, l, acc

    init = (jnp.full((tq, 1), NEG_BIG, F32), jnp.zeros((tq, 1), F32), jnp.zeros((tq, HEAD_DIM), F32))
    n_full = t0 // tk
    carry = init
    for kt in range(n_full):
        carry = step(kt, carry, False)
    for d in range(tq // tk):
        carry = step(n_full + d, carry, True)
    _, l, acc = carry
    o_ref[...] = (acc / jnp.maximum(l, 1e-30) * zs_ref[...].astype(F32)).astype(o_ref.dtype)


def _fox_prompt_attn(q_bf, kv_bf, cum, zs, tq=512, tk=512):
    B, T, W = q_bf.shape
    H = N_HEADS
    tq, tk = min(tq, T), min(tk, T)
    assert T % tq == 0 and tq % tk == 0
    out = None
    for qi in range(T // tq):
        n_keys = (qi + 1) * tq
        tile = pl.BlockSpec((None, tq, HEAD_DIM), lambda b, h, qi=qi: (b, qi, h))
        in_specs = [
            tile,
            pl.BlockSpec((None, n_keys, HEAD_DIM), lambda b, h: (b, 0, h)),
            pl.BlockSpec((None, n_keys, HEAD_DIM), lambda b, h: (b, 0, H + h)),
            pl.BlockSpec((None, None, 1, n_keys), lambda b, h: (b, h, 0, 0)),
            tile,
        ]
        args = [q_bf, kv_bf, kv_bf, cum, zs]
        aliases = {}
        if out is not None:
            in_specs.append(pl.BlockSpec(memory_space=pl.ANY))
            args.append(out)
            aliases = {len(args) - 1: 0}
        out = pl.pallas_call(
            functools.partial(_fox_prompt_kernel, tq, tk, qi),
            grid=(B, H),
            in_specs=in_specs,
            out_specs=tile,
            out_shape=jax.ShapeDtypeStruct((B, T, W), BF16),
            input_output_aliases=aliases,
            compiler_params=_cparams(("parallel", "parallel")),
            name="fox_prompt_attn",
        )(*args)
    return out


def _fox_sample_kernel(n_pg, page, T, pt_ref, q_ref, *refs):
    blocks = refs[:4 * n_pg]
    cum_ref, kvn_ref, lfn_ref, zs_ref, o_ref, m_ref, l_ref, acc_ref = refs[4 * n_pg:]
    c = pl.program_id(1)
    H = N_HEADS
    n_half = H // SUBLANES

    def head_rows(kv, h):
        return jnp.concatenate(
            [_tile_row(blocks[(p * 2 + kv) * n_half + h // SUBLANES], h % SUBLANES) for p in range(n_pg)], axis=0)

    q = q_ref[...].astype(BF16)

    @pl.when(c == 0)
    def _():
        m_ref[...] = jnp.full(m_ref.shape, NEG_BIG, F32)
        l_ref[...] = jnp.zeros(l_ref.shape, F32)
        acc_ref[...] = jnp.zeros(acc_ref.shape, F32)

    def attend(ks, vs, cum_rows, ok):
        n = ks[0].shape[0]
        s = jnp.concatenate(
            [lax.dot_general(q[:, h * HEAD_DIM:(h + 1) * HEAD_DIM], ks[h], NT, preferred_element_type=F32)
             - jnp.broadcast_to(cum_rows[h], (T, n)) for h in range(H)], axis=0)
        if ok is not None:
            ok = jnp.concatenate([ok] * H, axis=0)
            s = jnp.where(ok, s, NEG_BIG)
        m_new = jnp.maximum(m_ref[:, 0:1], jnp.max(s, axis=-1, keepdims=True))
        alpha = jnp.exp(m_ref[:, 0:1] - m_new)
        p = jnp.exp(s - m_new)
        if ok is not None:
            p = jnp.where(ok, p, 0.0)
        l_new = alpha * l_ref[:, 0:1] + jnp.sum(p, axis=-1, keepdims=True)
        pv = jnp.concatenate(
            [jnp.dot(p[h * T:(h + 1) * T].astype(BF16), vs[h], preferred_element_type=F32) for h in range(H)], axis=0)
        acc_ref[...] = alpha * acc_ref[...] + pv
        m_ref[...] = jnp.broadcast_to(m_new, (H * T, LANES))
        l_ref[...] = jnp.broadcast_to(l_new, (H * T, LANES))

    attend([head_rows(0, h).astype(BF16) for h in range(H)], [head_rows(1, h).astype(BF16) for h in range(H)],
           [cum_ref[h:h + 1, :] for h in range(H)], None)

    @pl.when(c == pl.num_programs(1) - 1)
    def _():
        total = cum_ref[:, n_pg * page - 1:n_pg * page]
        pad = jnp.zeros((LANES - T, LANES), F32)
        acc = jnp.zeros((LANES, LANES), F32)
        lower = _lower_ones(LANES)
        for part in _split3(jnp.concatenate([lfn_ref[...], pad], axis=0)):
            acc = acc + jnp.dot(lower, part, preferred_element_type=F32)
        cum_new = acc.T[:H] + total
        kvn = kvn_ref[...]
        zs = zs_ref[...]
        ok_n = lax.broadcasted_iota(jnp.int32, (1, LANES), 1) <= lax.broadcasted_iota(jnp.int32, (T, 1), 0)
        new = lambda j: jnp.concatenate([kvn[:, j * HEAD_DIM:(j + 1) * HEAD_DIM], pad], axis=0).astype(BF16)
        attend([new(h) for h in range(H)], [new(H + h) for h in range(H)],
               [cum_new[h:h + 1, :] for h in range(H)], ok_n)
        o = acc_ref[...] / jnp.maximum(l_ref[:, 0:1], 1e-30)
        o_ref[...] = jnp.concatenate(
            [o[h * T:(h + 1) * T] * zs[:, h * HEAD_DIM:(h + 1) * HEAD_DIM] for h in range(H)], axis=1)


def _fox_sample_attn(q, kv_pool, layer, page_table, cum, kv_new, logf_new, zs, B, T):
    page = kv_pool.shape[2]
    n_pages = page_table.shape[1]
    n_pg = FOX_PAGES
    H = N_HEADS
    W = H * HEAD_DIM
    assert n_pages % n_pg == 0 and T % SUBLANES == 0 and T <= LANES and H % SUBLANES == 0
    n_half = H // SUBLANES
    block_spec = lambda p, kv, half: pl.BlockSpec(
        (None, None, page, None, SUBLANES, HEAD_DIM),
        lambda b, c, pt: (layer, pt[b, c * n_pg + p], 0, kv, half, 0))
    block_specs = [block_spec(p, kv, half) for p in range(n_pg) for kv in range(2) for half in range(n_half)]
    tok = lambda width: pl.BlockSpec((T, width), lambda b, c, pt: (b, 0))
    return pl.pallas_call(
        functools.partial(_fox_sample_kernel, n_pg, page, T),
        grid_spec=pltpu.PrefetchScalarGridSpec(
            num_scalar_prefetch=1,
            grid=(B, n_pages // n_pg),
            in_specs=[tok(W)] + block_specs + [
                pl.BlockSpec((None, H, n_pg * page), lambda b, c, pt: (b, 0, c)),
                tok(2 * W), tok(LANES), tok(W),
            ],
            out_specs=tok(W),
            scratch_shapes=[
                pltpu.VMEM((H * T, LANES), F32), pltpu.VMEM((H * T, LANES), F32), pltpu.VMEM((H * T, HEAD_DIM), F32),
            ],
        ),
        out_shape=jax.ShapeDtypeStruct((B * T, W), F32),
        compiler_params=_cparams(("parallel", "arbitrary")),
        name="fox_sample_attn",
    )(page_table, q, *([kv_pool] * len(block_specs)), cum, kv_new, logf_new, zs)


def _alibi_slopes():
    s = np.exp2(-8.0 * np.arange(1, N_HEADS + 1) / N_HEADS)
    return jnp.asarray(s, dtype=F32).reshape(A_KV_GROUPS, A_REP)


_NSA_PLAN = (
    (0, 4, "normq", 0, (0,)),
    (4, 6, "raw", 0, (1,)),
    (6, 7, "norm", 2, (2, 4)),
    (7, 8, "raw", 0, (2, 4)),
    (8, 9, "norm", 3, (3, 4)),
    (9, 10, "raw", 0, (3, 4)),
    (10, 14, "silu", 0, (5,)),
)


def _nsa_outs(act_dtype):
    return ((0, 4, act_dtype), (4, 2, F32), (6, 2, F32), (8, 2, F32), (6, 4, act_dtype), (10, 4, act_dtype))


def _nsa_layer(xp, xs, norm_g, w_in, q_gain, k_gain, pe, w1, w2, w_out, slopes, past):
    mix = N_HEADS * HEAD_DIM
    kvw = A_KV_GROUPS * HEAD_DIM
    g0 = mix + 6 * kvw
    g1 = g0 + 3 * N_HEADS
    w_main = jnp.concatenate([w_in[:, :g0], w_in[:, g1:]], axis=1).astype(BF16)
    w_small = jnp.pad(w_in[:, g0:g1], ((0, 0), (0, LANES - 3 * N_HEADS))).astype(BF16)
    gains = jnp.concatenate([q_gain[None], k_gain, jnp.zeros((4, HEAD_DIM), F32)], axis=0)
    zero_bias = jnp.zeros((1, LANES), F32)
    w_out_bf = w_out.astype(BF16)
    slope_tab = jnp.broadcast_to(
        jnp.pad(slopes, ((0, 0), (0, 8 - A_REP)))[:, :, None], (A_KV_GROUPS, 8, LANES))
    shp = lambda B: (B, -1, 2, A_KV_GROUPS, HEAD_DIM)

    B, T, D = xp.shape
    M = B * T
    x2d = xp.reshape(M, D)
    q_bf, cmp, sel, win, rows_bf, zs, gates = _proj(x2d, norm_g, gains, zero_bias, w_main, w_small, _NSA_PLAN,
                                                    _nsa_outs(BF16), "sigmoid", PROJ_ROWS)
    kvc = _compress_prompt(cmp.reshape(B, T, -1), pe, w1.astype(BF16), w2.astype(BF16), k_gain[0])
    gates_g = gates[:, :3 * N_HEADS].reshape(B, T, 3, A_KV_GROUPS, A_REP)
    gates_g = jnp.transpose(gates_g, (0, 3, 1, 2, 4)).reshape(B, A_KV_GROUPS, T, 3 * A_REP)
    n_cmp = T // CMP_BLOCK
    expand = (jnp.arange(T, dtype=jnp.int32)[None, :] // SEL_BLOCK
              == jnp.arange(n_cmp, dtype=jnp.int32)[:, None]).astype(BF16)
    u = _nsa_prompt_attn(q_bf.reshape(B, T, -1), kvc, rows_bf.reshape(B, T, -1), gates_g,
                         zs.reshape(B, T, -1), slope_tab, expand)
    yp = _outproj(u.reshape(M, -1), w_out_bf, x2d, 512).reshape(B, T, D)
    out_p = (cmp.reshape(shp(B)), sel.reshape(shp(B)),
             win.reshape(B, T, -1)[:, T - min(WINDOW, T):].reshape(shp(B)))

    cmp_pool, sel_pool, layer, page_table, win_buf = past
    B, T, D = xs.shape
    M = B * T
    x2d = xs.reshape(M, D)
    q, cmp, sel, win, _, zs, gates = _proj(x2d, norm_g, gains, zero_bias, w_main, w_small, _NSA_PLAN,
                                           _nsa_outs(F32), "sigmoid", M)
    kvc = _compress_paged(cmp_pool, layer, page_table, pe, w1, w2, k_gain[0])
    rows = jnp.concatenate([cmp, sel, win], axis=1)
    u = _nsa_sample_attn(q, kvc, sel_pool, layer, page_table, rows, win_buf, gates, zs, slope_tab, B, T)
    ys = _outproj(u, w_out_bf, x2d, M).reshape(B, T, D)
    out_s = (cmp.reshape(shp(B)), sel.reshape(shp(B)),
             jnp.concatenate([win_buf[:, T:], win.reshape(shp(B))], axis=1))
    return yp, ys, out_p, out_s


_FOX_PLAN = (
    (0, 4, "normq", 0, (0,)),
    (4, 8, "norm", 1, (1, 2)),
    (8, 12, "raw", 0, (1, 2)),
    (12, 16, "silu", 0, (3,)),
)


def _fox_outs(act_dtype):
    return ((0, 4, act_dtype), (4, 8, F32), (4, 8, act_dtype), (12, 4, act_dtype))


def _fox_layer(xp, xs, norm_g, w_in, f_bias, q_gain, k_gain, w_out, past):
    mix = N_HEADS * HEAD_DIM
    w_main = jnp.concatenate([w_in[:, :3 * mix], w_in[:, 3 * mix + N_HEADS:]], axis=1).astype(BF16)
    w_small = jnp.pad(w_in[:, 3 * mix:3 * mix + N_HEADS], ((0, 0), (0, LANES - N_HEADS))).astype(BF16)
    gains = jnp.concatenate([q_gain[None], k_gain[None], jnp.zeros((6, HEAD_DIM), F32)], axis=0)
    sbias = jnp.pad(f_bias, (0, LANES - N_HEADS)).reshape(1, LANES)
    w_out_bf = w_out.astype(BF16)

    B, T, D = xp.shape
    M = B * T
    x2d = xp.reshape(M, D)
    q_bf, kv, kv_bf, zs, logf = _proj(x2d, norm_g, gains, sbias, w_main, w_small, _FOX_PLAN, _fox_outs(BF16),
                                      "logsig", PROJ_ROWS)
    cum = _cumsum_heads(logf.reshape(B, T, LANES))[:, :N_HEADS].reshape(B, N_HEADS, 1, T)
    u = _fox_prompt_attn(q_bf.reshape(B, T, -1), kv_bf.reshape(B, T, -1), cum, zs.reshape(B, T, -1))
    yp = _outproj(u.reshape(M, -1), w_out_bf, x2d, 512).reshape(B, T, D)
    out_p = (kv.reshape(B, T, 2, N_HEADS, HEAD_DIM), logf[:, :N_HEADS].reshape(B, T, N_HEADS))

    kv_pool, logf_pool, layer, page_table = past
    B, T, D = xs.shape
    M = B * T
    x2d = xs.reshape(M, D)
    q, kv, _, zs, logf = _proj(x2d, norm_g, gains, sbias, w_main, w_small, _FOX_PLAN, _fox_outs(F32),
                               "logsig", M)
    cum = _cumsum_paged(logf_pool, layer, page_table)
    u = _fox_sample_attn(q, kv_pool, layer, page_table, cum, kv, logf, zs, B, T)
    ys = _outproj(u, w_out_bf, x2d, M).reshape(B, T, D)
    out_s = (kv.reshape(B, T, 2, N_HEADS, HEAD_DIM), logf[:, :N_HEADS].reshape(B, T, N_HEADS))
    return yp, ys, out_p, out_s


def kernel(x_prompt, x_sample, cache_a_cmp, cache_a_sel, state_a_win, cache_b_kv, cache_b_logf, page_table,
           a_norm, a_w_in, a_q_gain, a_k_gain, a_phi_pe, a_phi_w1, a_phi_w2, a_w_out,
           b_norm, b_w_in, b_f_bias, b_q_gain, b_k_gain, b_w_out):
    slopes = _alibi_slopes()
    xp, xs = x_prompt, x_sample
    depth = a_norm.shape[0] + b_norm.shape[0]
    a_p, a_s, b_p, b_s = [], [], [], []
    for i in range(depth):
        j = i // 2
        if i % 2 == 0:
            xp, xs, out_p, out_s = _nsa_layer(
                xp, xs, a_norm[j], a_w_in[j], a_q_gain[j], a_k_gain[j], a_phi_pe[j], a_phi_w1[j], a_phi_w2[j],
                a_w_out[j], slopes, (cache_a_cmp, cache_a_sel, j, page_table, state_a_win[j]))
            a_p.append(out_p)
            a_s.append(out_s)
        else:
            xp, xs, out_p, out_s = _fox_layer(
                xp, xs, b_norm[j], b_w_in[j], b_f_bias[j], b_q_gain[j], b_k_gain[j], b_w_out[j],
                (cache_b_kv, cache_b_logf, j, page_table))
            b_p.append(out_p)
            b_s.append(out_s)
    stack = lambda outs, k: jnp.stack([o[k] for o in outs])
    return (xp, xs,
            stack(a_p, 0), stack(a_s, 0), stack(a_p, 1), stack(a_s, 1), stack(a_p, 2), stack(a_s, 2),
            stack(b_p, 0), stack(b_s, 0), stack(b_p, 1), stack(b_s, 1))
```

```python
import functools

import jax
import jax.numpy as jnp
import numpy as np
from jax import lax
from jax.experimental import pallas as pl
from jax.experimental.pallas import tpu as pltpu

HEAD_DIM = 128
N_HEADS = 16
A_KV_GROUPS = 4
A_REP = N_HEADS // A_KV_GROUPS
CMP_BLOCK = 64
SEL_BLOCK = CMP_BLOCK
TOP_N = 16
WINDOW = 512
RMS_EPS = 1e-6
SCALE = HEAD_DIM ** -0.5
FORCED_SCORE = float(A_REP + 1)
NEG_BIG = -1e30

LANES = 128
SUBLANES = 8
PROJ_COLS = 512
PROJ_ROWS = 1024
VMEM_LIMIT = 56 * 1024 * 1024
CMP_PAGES = 16
SEL_PAGES = 8
FOX_PAGES = 4
LOGF_PAGES = 8
ROW_CHUNK = 128

F32 = jnp.float32
BF16 = jnp.bfloat16
NT = (((1,), (1,)), ((), ()))
TN = (((0,), (0,)), ((), ()))


def _cparams(sem):
    return pltpu.CompilerParams(dimension_semantics=sem, vmem_limit_bytes=VMEM_LIMIT)


def _head_rms(a, gain):
    ms = jnp.mean(a * a, axis=-1, keepdims=True)
    return a * lax.rsqrt(ms + RMS_EPS) * gain


def _softmax_rows(s, ok):
    m = jnp.max(s, axis=-1, keepdims=True)
    p = jnp.where(ok, jnp.exp(s - m), 0.0)
    return p / jnp.maximum(jnp.sum(p, axis=-1, keepdims=True), 1e-30)


def _split3(x):
    hi = x.astype(BF16)
    r1 = x - hi.astype(F32)
    mid = r1.astype(BF16)
    lo = (r1 - mid.astype(F32)).astype(BF16)
    return hi, mid, lo


def _lower_ones(n):
    tri = lax.broadcasted_iota(jnp.int32, (n, n), 0) >= lax.broadcasted_iota(jnp.int32, (n, n), 1)
    return jnp.where(tri, 1.0, 0.0).astype(BF16)


def _tile_row(block_ref, j):
    n, _, d = block_ref.shape
    return block_ref.reshape(n * SUBLANES, d)[pl.ds(j, n, stride=SUBLANES), :]


def _rank_counts(score, blk, n_blk):
    rank = jnp.zeros(score.shape, F32)
    for i in range(n_blk):
        col = score[:, i:i + 1]
        rank = rank + jnp.where(blk > i, jnp.where(col >= score, 1.0, 0.0), jnp.where(col > score, 1.0, 0.0))
    return rank


def _proj_kernel(plan, small_kind, n_in, *refs):
    x_ref, ng_ref, gains_ref, sbias_ref, w_ref, ws_ref = refs[:6]
    refs = refs[n_in:]
    n_out = len(refs) - 1
    outs, xn_ref = refs[:n_out], refs[n_out]
    j = pl.program_id(1)

    @pl.when(j == 0)
    def _():
        x = x_ref[...]
        ms = jnp.mean(x * x, axis=-1, keepdims=True)
        xn_ref[...] = (x * lax.rsqrt(ms + RMS_EPS) * ng_ref[...]).astype(BF16)
        small = jnp.dot(xn_ref[...], ws_ref[...], preferred_element_type=F32)
        if small_kind == "sigmoid":
            small = jax.nn.sigmoid(small)
        else:
            small = small + sbias_ref[...]
            small = jnp.minimum(small, 0.0) - jnp.log1p(jnp.exp(-jnp.abs(small)))
        outs[n_out - 1][...] = small

    for (j0, j1, kind, gain_row, targets) in plan:
        @pl.when((j >= j0) & (j < j1))
        def _(kind=kind, gain_row=gain_row, targets=targets):
            acc = jnp.dot(xn_ref[...], w_ref[...], preferred_element_type=F32)
            if kind in ("norm", "normq"):
                g = gains_ref[gain_row:gain_row + 1, :]
                if kind == "normq":
                    g = g * SCALE
                acc = jnp.concatenate(
                    [_head_rms(acc[:, h * LANES:(h + 1) * LANES], g) for h in range(PROJ_COLS // LANES)], axis=1)
            elif kind == "silu":
                acc = acc * jax.nn.sigmoid(acc)
            for t in targets:
                outs[t][...] = acc.astype(outs[t].dtype)


def _proj(x2d, norm_gain, gains, sbias, w_main, w_small, plan, out_defs, small_kind, bm, stacked=None):
    M, D = x2d.shape
    nblk = w_main.shape[1] // PROJ_COLS
    grid = (M // bm, nblk)

    def out_map(j0, n):
        return lambda i, j: (i, jnp.clip(j - j0, 0, n - 1))

    out_shapes = [jax.ShapeDtypeStruct((M, n * PROJ_COLS), dt) for (_, n, dt) in out_defs]
    out_specs = [pl.BlockSpec((bm, PROJ_COLS), out_map(j0, n)) for (j0, n, _) in out_defs]
    out_shapes.append(jax.ShapeDtypeStruct((M, LANES), F32))
    out_specs.append(pl.BlockSpec((bm, LANES), lambda i, j: (i, 0)))
    in_specs = [
        pl.BlockSpec((bm, D), lambda i, j: (i, 0)),
        pl.BlockSpec((1, D), lambda i, j: (0, 0)),
        pl.BlockSpec((8, LANES), lambda i, j: (0, 0)),
        pl.BlockSpec((1, LANES), lambda i, j: (0, 0)),
        pl.BlockSpec((D, PROJ_COLS), lambda i, j: (0, j)),
        pl.BlockSpec((D, LANES), lambda i, j: (0, 0)),
    ]
    args = [x2d, norm_gain.reshape(1, D), gains, sbias, w_main, w_small]
    aliases = {}
    if stacked is not None:
        t, layer, n_layers, buf = stacked
        j0, n, dt = out_defs[t]
        out_shapes[t] = jax.ShapeDtypeStruct((n_layers, M, n * PROJ_COLS), dt)
        out_specs[t] = pl.BlockSpec(
            (None, bm, PROJ_COLS), lambda i, j: (layer, i, jnp.clip(j - j0, 0, n - 1)))
        if buf is not None:
            in_specs.append(pl.BlockSpec(memory_space=pl.ANY))
            args.append(buf)
            aliases = {len(args) - 1: t}
    return pl.pallas_call(
        functools.partial(_proj_kernel, plan, small_kind, len(args)),
        grid=grid,
        in_specs=in_specs,
        out_specs=out_specs,
        out_shape=out_shapes,
        input_output_aliases=aliases,
        scratch_shapes=[pltpu.VMEM((bm, D), BF16)],
        compiler_params=_cparams(("parallel", "arbitrary")),
        name="proj",
    )(*args)


def _outproj_kernel(u_ref, w_ref, x_ref, o_ref):
    o_ref[...] = x_ref[...] + jnp.dot(u_ref[...].astype(BF16), w_ref[...], preferred_element_type=F32)


def _outproj(u2d, w_bf, x2d, bm):
    M, K = u2d.shape
    N = w_bf.shape[1]
    return pl.pallas_call(
        _outproj_kernel,
        grid=(M // bm,),
        in_specs=[
            pl.BlockSpec((bm, K), lambda i: (i, 0)),
            pl.BlockSpec((K, N), lambda i: (0, 0)),
            pl.BlockSpec((bm, N), lambda i: (i, 0)),
        ],
        out_specs=pl.BlockSpec((bm, N), lambda i: (i, 0)),
        out_shape=jax.ShapeDtypeStruct((M, N), F32),
        compiler_params=_cparams(("parallel",)),
        name="outproj",
    )(u2d, w_bf, x2d)


def _compress_kernel(n_blk, x0_ref, x1_ref, x2_ref, x3_ref, pe_ref, w1_ref, w2_ref, kg_ref, o_ref):
    kv = pl.program_id(1)

    def body(i, acc):
        pe = pe_ref[pl.ds(i, 1), :]
        a = jnp.concatenate(
            [x_ref[pl.ds(i, n_blk, stride=CMP_BLOCK), :] + pe for x_ref in (x0_ref, x1_ref, x2_ref, x3_ref)],
            axis=0)
        return acc + jnp.dot(a.astype(BF16), w1_ref[i], preferred_element_type=F32)

    acc = lax.fori_loop(0, CMP_BLOCK, body, jnp.zeros((A_KV_GROUPS * n_blk, HEAD_DIM), F32))
    h = jax.nn.gelu(acc)
    out = jnp.dot(h.astype(BF16), w2_ref[...], preferred_element_type=F32)
    out = jnp.where(kv == 0, _head_rms(out, kg_ref[...]), out)
    for g in range(A_KV_GROUPS):
        o_ref[g] = out[g * n_blk:(g + 1) * n_blk].astype(o_ref.dtype)


def _compress_prompt(rows, pe, w1_bf, w2_bf, k_gain_cmp):
    B, T = rows.shape[0], rows.shape[1]
    n_blk = T // CMP_BLOCK
    G = A_KV_GROUPS
    grp = lambda g: pl.BlockSpec((None, T, HEAD_DIM), lambda b, kv, g=g: (b, 0, kv * G + g))
    return pl.pallas_call(
        functools.partial(_compress_kernel, n_blk),
        grid=(B, 2),
        in_specs=[
            grp(0), grp(1), grp(2), grp(3),
            pl.BlockSpec((None, CMP_BLOCK, HEAD_DIM), lambda b, kv: (kv, 0, 0)),
            pl.BlockSpec((None, CMP_BLOCK, HEAD_DIM, HEAD_DIM), lambda b, kv: (kv, 0, 0, 0)),
            pl.BlockSpec((None, HEAD_DIM, HEAD_DIM), lambda b, kv: (kv, 0, 0)),
            pl.BlockSpec((1, HEAD_DIM), lambda b, kv: (0, 0)),
        ],
        out_specs=pl.BlockSpec((None, None, A_KV_GROUPS, n_blk, HEAD_DIM), lambda b, kv: (b, kv, 0, 0, 0)),
        out_shape=jax.ShapeDtypeStruct((B, 2, A_KV_GROUPS, n_blk, HEAD_DIM), BF16),
        compiler_params=_cparams(("parallel", "arbitrary")),
        name="compress",
    )(rows, rows, rows, rows, pe, w1_bf, w2_bf, k_gain_cmp.reshape(1, HEAD_DIM))


def _compress_paged_kernel(n_pg, bpp, pt_ref, *refs):
    pages = refs[:n_pg]
    pe_ref, w1_ref, w2_ref, kg_ref, o_ref = refs[n_pg:]
    n_rows = n_pg * bpp * SUBLANES
    slabs = []
    for i in range(CMP_BLOCK):
        pe = pe_ref[i]
        a = jnp.concatenate([pg[h * CMP_BLOCK + i] + pe for pg in pages for h in range(bpp)], axis=0)
        slabs.append(a.astype(BF16))
    acc = jnp.dot(jnp.concatenate(slabs, axis=1), w1_ref[...], preferred_element_type=F32)
    h = jax.nn.gelu(acc)
    out_k = jnp.dot(h[:, :HEAD_DIM].astype(BF16), w2_ref[0], preferred_element_type=F32)
    out_v = jnp.dot(h[:, HEAD_DIM:].astype(BF16), w2_ref[1], preferred_element_type=F32)
    out_k = _head_rms(out_k, kg_ref[...])
    is_k = (lax.broadcasted_iota(jnp.int32, (n_rows, 1), 0) & (SUBLANES - 1)) < A_KV_GROUPS
    o_ref[...] = jnp.where(is_k, out_k, out_v)


def _compress_paged(pool, layer, page_table, pe, w1, w2, k_gain_cmp):
    page = pool.shape[2]
    bpp = page // CMP_BLOCK
    B, n_pages = page_table.shape
    n_pg = CMP_PAGES
    assert page % CMP_BLOCK == 0 and n_pages % n_pg == 0 and 2 * A_KV_GROUPS == SUBLANES
    pool5 = pool.reshape(pool.shape[0], pool.shape[1], page, SUBLANES, HEAD_DIM)
    pe8 = jnp.repeat(jnp.transpose(pe, (1, 0, 2)), A_KV_GROUPS, axis=1)
    w1cat = jnp.concatenate([w1[0], w1[1]], axis=-1).astype(BF16)
    w1cat = w1cat.reshape(CMP_BLOCK * HEAD_DIM, 2 * HEAD_DIM)
    rows_per_step = n_pg * bpp * SUBLANES
    page_spec = lambda p: pl.BlockSpec(
        (None, None, page, SUBLANES, HEAD_DIM), lambda b, c, pt, p=p: (layer, pt[b, c * n_pg + p], 0, 0, 0))
    return pl.pallas_call(
        functools.partial(_compress_paged_kernel, n_pg, bpp),
        grid_spec=pltpu.PrefetchScalarGridSpec(
            num_scalar_prefetch=1,
            grid=(B, n_pages // n_pg),
            in_specs=[page_spec(p) for p in range(n_pg)] + [
                pl.BlockSpec((CMP_BLOCK, SUBLANES, HEAD_DIM), lambda b, c, pt: (0, 0, 0)),
                pl.BlockSpec((CMP_BLOCK * HEAD_DIM, 2 * HEAD_DIM), lambda b, c, pt: (0, 0)),
                pl.BlockSpec((2, HEAD_DIM, HEAD_DIM), lambda b, c, pt: (0, 0, 0)),
                pl.BlockSpec((1, HEAD_DIM), lambda b, c, pt: (0, 0)),
            ],
            out_specs=pl.BlockSpec((None, rows_per_step, HEAD_DIM), lambda b, c, pt: (b, c, 0)),
        ),
        out_shape=jax.ShapeDtypeStruct((B, n_pages * bpp * SUBLANES, HEAD_DIM), F32),
        compiler_params=_cparams(("parallel", "arbitrary")),
        name="compress_paged",
    )(page_table, *([pool5] * n_pg), pe8, w1cat, w2.astype(BF16), k_gain_cmp.reshape(1, HEAD_DIM))


def _select_blocks_t(imp_t, cur, n_blk):
    blk = lax.broadcasted_iota(jnp.int32, (n_blk, 1), 0)
    valid = blk <= cur
    forced = (blk == 0) | (blk == cur) | (blk == cur - 1)
    score = jnp.where(valid, jnp.where(forced, FORCED_SCORE, imp_t), -jnp.inf)
    sub = lax.broadcasted_iota(jnp.int32, (SUBLANES, 1), 0)
    ranks = []
    for b0 in range(0, n_blk, SUBLANES):
        slab = score[b0:b0 + SUBLANES]
        rank = jnp.zeros(slab.shape, F32)
        for i in range(n_blk):
            row = score[i:i + 1]
            if i < b0:
                rank = rank + jnp.where(row >= slab, 1.0, 0.0)
            elif i >= b0 + SUBLANES:
                rank = rank + jnp.where(row > slab, 1.0, 0.0)
            else:
                rank = rank + jnp.where(sub > i - b0, jnp.where(row >= slab, 1.0, 0.0),
                                        jnp.where(row > slab, 1.0, 0.0))
        ranks.append(rank)
    rank = jnp.concatenate(ranks, axis=0)
    return jnp.where(valid & (rank < float(min(TOP_N, n_blk))), 1.0, 0.0)


def _nsa_prompt_kernel(tq, tk, T, n_full, q_ref, kc_ref, vc_ref, ks_ref, vs_ref, kw_ref, vw_ref, gt_ref, zs_ref,
                       slope_ref, exp_ref, *rest):
    o_ref = rest[-1]
    qi = n_full * (tk // tq) + pl.program_id(2)
    t0 = qi * tq
    n_cmp = T // CMP_BLOCK
    R = A_REP
    q = q_ref[...]
    q4 = jnp.concatenate([q[:, r * HEAD_DIM:(r + 1) * HEAD_DIM] for r in range(R)], axis=0)
    qpos = t0 + lax.broadcasted_iota(jnp.int32, (tq, 1), 0)
    qpos_row = t0 + lax.broadcasted_iota(jnp.int32, (1, tq), 1)
    slopes = [slope_ref[r:r + 1, 0:1] for r in range(R)]

    sct = lax.dot_general(kc_ref[...], q4, NT, preferred_element_type=F32)
    c_end = (lax.broadcasted_iota(jnp.int32, (n_cmp, 1), 0) + 1) * CMP_BLOCK - 1
    ok_c = qpos_row >= c_end
    cposf = (c_end - t0).astype(F32)
    p_parts = []
    for r in range(R):
        s_r = jnp.where(ok_c, sct[:, r * tq:(r + 1) * tq] + slopes[r] * cposf, NEG_BIG)
        p_r = jnp.where(ok_c, jnp.exp(s_r - jnp.max(s_r, axis=0, keepdims=True)), 0.0)
        p_parts.append(p_r / jnp.maximum(jnp.sum(p_r, axis=0, keepdims=True), 1e-30))
    p_ct = jnp.concatenate(p_parts, axis=1).astype(BF16)
    o_cmp = lax.dot_general(p_ct, vc_ref[...], TN, preferred_element_type=F32)
    imp_t = p_parts[0]
    for r in range(1, R):
        imp_t = imp_t + p_parts[r]
    sel_t = _select_blocks_t(imp_t, qpos_row // SEL_BLOCK, n_cmp).astype(BF16)

    def sel_tile(kt, carry, diagonal):
        m, l, acc = carry
        k0 = kt * tk
        s = lax.dot_general(q4, ks_ref[k0:k0 + tk, :], NT, preferred_element_type=F32)
        v = vs_ref[k0:k0 + tk, :]
        kpos = k0 + lax.broadcasted_iota(jnp.int32, (1, tk), 1)
        picked = lax.dot_general(sel_t, exp_ref[:, k0:k0 + tk], TN, preferred_element_type=F32)
        mask = (picked - 1.0) * (-NEG_BIG)
        if diagonal:
            mask = jnp.where(kpos <= qpos, mask, NEG_BIG)
        kposf = (kpos - t0).astype(F32)
        s = jnp.concatenate(
            [s[r * tq:(r + 1) * tq] + (mask + slopes[r] * kposf) for r in range(R)], axis=0)
        m_new = jnp.maximum(m, jnp.max(s, axis=-1, keepdims=True))
        alpha = jnp.exp(m - m_new)
        p = jnp.exp(s - m_new)
        l = alpha * l + jnp.sum(p, axis=-1, keepdims=True)
        acc = alpha * acc + jnp.dot(p.astype(BF16), v, preferred_element_type=F32)
        return m_new, l, acc

    carry = (jnp.full((R * tq, 1), NEG_BIG, F32), jnp.zeros((R * tq, 1), F32), jnp.zeros((R * tq, HEAD_DIM), F32))
    for kt in range(n_full):
        carry = sel_tile(kt, carry, False)
    _, l_s, acc_s = sel_tile(n_full, carry, True)
    o_sel = acc_s / l_s

    span = WINDOW + tq
    w0 = pl.multiple_of(jnp.maximum(t0 - WINDOW, 0), tq)
    kw = kw_ref[pl.ds(w0, span), :]
    vw = vw_ref[pl.ds(w0, span), :]
    sw = lax.dot_general(q4, kw, NT, preferred_element_type=F32)
    wpos = w0 + lax.broadcasted_iota(jnp.int32, (1, span), 1)
    d_w = qpos - wpos
    mask_w = jnp.where((d_w >= 0) & (d_w < WINDOW), 0.0, NEG_BIG)
    wposf = (wpos - t0).astype(F32)
    sw = jnp.concatenate(
        [sw[r * tq:(r + 1) * tq] + (mask_w + slopes[r] * wposf) for r in range(R)], axis=0)
    pw = jnp.exp(sw - jnp.max(sw, axis=-1, keepdims=True))
    o_win = (jnp.dot(pw.astype(BF16), vw, preferred_element_type=F32)
             / jnp.sum(pw, axis=-1, keepdims=True))

    gt = gt_ref[...]
    zs = zs_ref[...]
    outs = []
    for r in range(R):
        rows = slice(r * tq, (r + 1) * tq)
        o_r = (gt[:, r:r + 1] * o_cmp[rows] + gt[:, R + r:R + r + 1] * o_sel[rows]
               + gt[:, 2 * R + r:2 * R + r + 1] * o_win[rows])
        outs.append(o_r * zs[:, r * HEAD_DIM:(r + 1) * HEAD_DIM].astype(F32))
    o_ref[...] = jnp.concatenate(outs, axis=1).astype(o_ref.dtype)


def _nsa_prompt_attn(q_bf, kvc, rows_bf, gates_g, zs, slope_tab, expand, tq=256, tk=512):
    B, T, W = q_bf.shape
    G = A_KV_GROUPS
    gw = A_REP * HEAD_DIM
    n_cmp = T // CMP_BLOCK
    tk = min(tk, T)
    assert T % tk == 0 and tk % tq == 0 and T >= WINDOW + tq and tk >= WINDOW
    per_call = tk // tq
    out = None
    for n_full in range(T // tk):
        n_keys = (n_full + 1) * tk
        q0 = n_full * per_call
        n_win = max(n_keys, WINDOW + tq)
        prefix = lambda col, rows=n_keys: pl.BlockSpec(
            (None, rows, HEAD_DIM), lambda b, g, i, col=col: (b, 0, col * G + g))
        tile = pl.BlockSpec((None, tq, gw), lambda b, g, i, q0=q0: (b, q0 + i, g))
        in_specs = [
            tile,
            pl.BlockSpec((None, None, None, n_cmp, HEAD_DIM), lambda b, g, i: (b, 0, g, 0, 0)),
            pl.BlockSpec((None, None, None, n_cmp, HEAD_DIM), lambda b, g, i: (b, 1, g, 0, 0)),
            prefix(0), prefix(1), prefix(2, n_win), prefix(3, n_win),
            pl.BlockSpec((None, None, tq, 3 * A_REP), lambda b, g, i, q0=q0: (b, g, q0 + i, 0)),
            tile,
            pl.BlockSpec((None, 8, LANES), lambda b, g, i: (g, 0, 0)),
            pl.BlockSpec((n_cmp, n_keys), lambda b, g, i: (0, 0)),
        ]
        args = [q_bf, kvc, kvc, rows_bf, rows_bf, rows_bf, rows_bf, gates_g, zs, slope_tab, expand]
        aliases = {}
        if out is not None:
            in_specs.append(pl.BlockSpec(memory_space=pl.ANY))
            args.append(out)
            aliases = {len(args) - 1: 0}
        out = pl.pallas_call(
            functools.partial(_nsa_prompt_kernel, tq, tk, T, n_full),
            grid=(B, G, per_call),
            in_specs=in_specs,
            out_specs=tile,
            out_shape=jax.ShapeDtypeStruct((B, T, W), BF16),
            input_output_aliases=aliases,
            compiler_params=_cparams(("parallel", "parallel", "arbitrary")),
            name="nsa_prompt_attn",
        )(*args)
    return out


def _nsa_sample_kernel(n_pg, page, P_len, T, wb, pt_ref, q_ref, kvc_ref, *refs):
    pages = refs[:n_pg]
    rows_ref, win_ref, gt_ref, zs_ref, slope_ref, o_ref, m_ref, l_ref, acc_ref, sel_ref, ocmp_ref = refs[n_pg:]
    c = pl.program_id(1)
    G, R = A_KV_GROUPS, A_REP
    n_blk = P_len // SEL_BLOCK
    q = q_ref[...]
    tpos = lax.broadcasted_iota(jnp.int32, (T, 1), 0)

    def q4(g):
        return jnp.concatenate(
            [q[:, (g * R + r) * HEAD_DIM:(g * R + r + 1) * HEAD_DIM] for r in range(R)], axis=0).astype(BF16)

    def slope(g, r):
        return slope_ref[g, r:r + 1, 0:1]

    def flash_update(g, k, v, ok, kposf):
        s = lax.dot_general(q4(g), k, NT, preferred_element_type=F32)
        s = jnp.concatenate(
            [jnp.where(ok, s[r * T:(r + 1) * T] + slope(g, r) * kposf, NEG_BIG) for r in range(R)], axis=0)
        ok4 = jnp.concatenate([ok] * R, axis=0)
        m = m_ref[g][:, 0:1]
        m_new = jnp.maximum(m, jnp.max(s, axis=-1, keepdims=True))
        alpha = jnp.exp(m - m_new)
        p = jnp.where(ok4, jnp.exp(s - m_new), 0.0)
        l_new = alpha * l_ref[g][:, 0:1] + jnp.sum(p, axis=-1, keepdims=True)
        acc_ref[g] = alpha * acc_ref[g] + jnp.dot(p.astype(BF16), v, preferred_element_type=F32)
        m_ref[g] = jnp.broadcast_to(m_new, (R * T, LANES))
        l_ref[g] = jnp.broadcast_to(l_new, (R * T, LANES))

    @pl.when(c == 0)
    def _():
        c_end = (lax.broadcasted_iota(jnp.int32, (1, n_blk), 1) + 1) * CMP_BLOCK - 1
        cposf = (c_end - P_len).astype(F32)
        ok_c = (P_len + tpos) >= c_end
        imps = []
        for g in range(G):
            kc = kvc_ref[pl.ds(g, n_blk, stride=SUBLANES), :].astype(BF16)
            vc = kvc_ref[pl.ds(G + g, n_blk, stride=SUBLANES), :].astype(BF16)
            sc = lax.dot_general(q4(g), kc, NT, preferred_element_type=F32)
            parts = []
            for r in range(R):
                s_r = sc[r * T:(r + 1) * T] + slope(g, r) * cposf
                parts.append(_softmax_rows(jnp.where(ok_c, s_r, NEG_BIG), ok_c))
            ocmp_ref[g] = jnp.dot(jnp.concatenate(parts, axis=0).astype(BF16), vc, preferred_element_type=F32)
            imps.append(parts[0] + parts[1] + parts[2] + parts[3])
        imp = jnp.concatenate(imps, axis=0)
        blk = lax.broadcasted_iota(jnp.int32, (1, n_blk), 1)
        forced = (blk == 0) | (blk == n_blk - 1)
        score = jnp.where(forced, FORCED_SCORE, imp)
        rank = _rank_counts(score, blk, n_blk) + jnp.where(FORCED_SCORE > score, 1.0, 0.0)
        sel_ref[...] = jnp.where(rank < float(TOP_N), 1.0, 0.0)
        m_ref[...] = jnp.full(m_ref.shape, NEG_BIG, F32)
        l_ref[...] = jnp.zeros(l_ref.shape, F32)
        acc_ref[...] = jnp.zeros(acc_ref.shape, F32)

    n_keys = n_pg * page
    kpos = c * n_keys + lax.broadcasted_iota(jnp.int32, (1, n_keys), 1)
    kposf = (kpos - P_len).astype(F32)
    rowblk = lax.broadcasted_iota(jnp.int32, (n_blk, 1), 0)
    expand = jnp.where(rowblk == kpos // SEL_BLOCK, 1.0, 0.0).astype(BF16)
    picked = jnp.dot(sel_ref[...].astype(BF16), expand, preferred_element_type=F32)
    for g in range(G):
        k = jnp.concatenate([pg[pl.ds(g, page, stride=SUBLANES), :] for pg in pages], axis=0).astype(BF16)
        v = jnp.concatenate([pg[pl.ds(G + g, page, stride=SUBLANES), :] for pg in pages], axis=0).astype(BF16)
        flash_update(g, k, v, picked[g * T:(g + 1) * T] > 0.5, kposf)

    @pl.when(c == pl.num_programs(1) - 1)
    def _():
        kvw = G * HEAD_DIM
        rows = rows_ref[...]
        gt = gt_ref[...]
        zs = zs_ref[...]
        pad = jnp.zeros((LANES - T, HEAD_DIM), F32)
        lane = lax.broadcasted_iota(jnp.int32, (1, LANES), 1)
        widx = lax.broadcasted_iota(jnp.int32, (1, wb + LANES), 1)
        ok_w = (widx > tpos + (wb - WINDOW)) & (widx <= tpos + wb)
        wposf = (widx - wb).astype(F32)
        outs = []
        for g in range(G):
            new = lambda sec: jnp.concatenate(
                [rows[:, sec * kvw + g * HEAD_DIM:sec * kvw + (g + 1) * HEAD_DIM], pad], axis=0)
            flash_update(g, new(2).astype(BF16), new(3).astype(BF16), lane <= tpos, lane.astype(F32))
            o_sel = acc_ref[g] / jnp.maximum(l_ref[g][:, 0:1], 1e-30)
            kw = jnp.concatenate([win_ref[pl.ds(g, wb, stride=SUBLANES), :], new(4)], axis=0).astype(BF16)
            vw = jnp.concatenate([win_ref[pl.ds(G + g, wb, stride=SUBLANES), :], new(5)], axis=0).astype(BF16)
            sw = lax.dot_general(q4(g), kw, NT, preferred_element_type=F32)
            parts = []
            for r in range(R):
                s_r = sw[r * T:(r + 1) * T] + slope(g, r) * wposf
                parts.append(_softmax_rows(jnp.where(ok_w, s_r, NEG_BIG), ok_w))
            o_win = jnp.dot(jnp.concatenate(parts, axis=0).astype(BF16), vw, preferred_element_type=F32)
            o_cmp = ocmp_ref[g]
            for r in range(R):
                h = g * R + r
                rs = slice(r * T, (r + 1) * T)
                o_r = (gt[:, h:h + 1] * o_cmp[rs] + gt[:, N_HEADS + h:N_HEADS + h + 1] * o_sel[rs]
                       + gt[:, 2 * N_HEADS + h:2 * N_HEADS + h + 1] * o_win[rs])
                outs.append(o_r * zs[:, h * HEAD_DIM:(h + 1) * HEAD_DIM])
        o_ref[...] = jnp.concatenate(outs, axis=1)


def _nsa_sample_attn(q, kvc, sel_pool, layer, page_table, rows, win_buf, gates, zs, slope_tab, B, T):
    page = sel_pool.shape[2]
    n_pages = page_table.shape[1]
    P_len = n_pages * page
    wb = win_buf.shape[1]
    n_pg = SEL_PAGES
    n_blk = P_len // SEL_BLOCK
    G, R = A_KV_GROUPS, A_REP
    W = N_HEADS * HEAD_DIM
    assert P_len % SEL_BLOCK == 0 and T <= SEL_BLOCK and T % SUBLANES == 0 and n_blk + 1 > TOP_N
    assert n_pages % n_pg == 0 and WINDOW <= wb <= P_len and T <= LANES
    pool4 = sel_pool.reshape(sel_pool.shape[0], sel_pool.shape[1], page * SUBLANES, HEAD_DIM)
    win3 = win_buf.reshape(B, wb * SUBLANES, HEAD_DIM)
    page_spec = lambda p: pl.BlockSpec(
        (None, None, page * SUBLANES, HEAD_DIM), lambda b, c, pt, p=p: (layer, pt[b, c * n_pg + p], 0, 0))
    tok = lambda width: pl.BlockSpec((T, width), lambda b, c, pt: (b, 0))
    return pl.pallas_call(
        functools.partial(_nsa_sample_kernel, n_pg, page, P_len, T, wb),
        grid_spec=pltpu.PrefetchScalarGridSpec(
            num_scalar_prefetch=1,
            grid=(B, n_pages // n_pg),
            in_specs=[tok(W), pl.BlockSpec((None, n_blk * SUBLANES, HEAD_DIM), lambda b, c, pt: (b, 0, 0))]
            + [page_spec(p) for p in range(n_pg)] + [
                tok(rows.shape[1]),
                pl.BlockSpec((None, wb * SUBLANES, HEAD_DIM), lambda b, c, pt: (b, 0, 0)),
                tok(LANES), tok(W),
                pl.BlockSpec((G, 8, LANES), lambda b, c, pt: (0, 0, 0)),
            ],
            out_specs=tok(W),
            scratch_shapes=[
                pltpu.VMEM((G, R * T, LANES), F32), pltpu.VMEM((G, R * T, LANES), F32),
                pltpu.VMEM((G, R * T, HEAD_DIM), F32), pltpu.VMEM((G * T, n_blk), F32),
                pltpu.VMEM((G, R * T, HEAD_DIM), F32),
            ],
        ),
        out_shape=jax.ShapeDtypeStruct((B * T, W), F32),
        compiler_params=_cparams(("parallel", "arbitrary")),
        name="nsa_sample_attn",
    )(page_table, q, kvc, *([pool4] * n_pg), rows, win3, gates, zs, slope_tab)


def _cumsum_kernel(tb, lf_ref, o_ref, carry_ref):
    @pl.when(pl.program_id(1) == 0)
    def _():
        carry_ref[...] = jnp.zeros_like(carry_ref)

    lower = _lower_ones(tb)
    acc = jnp.zeros((tb, LANES), F32)
    for part in _split3(lf_ref[...]):
        acc = acc + jnp.dot(lower, part, preferred_element_type=F32)
    acc = acc.T + carry_ref[...]
    o_ref[...] = acc
    carry_ref[...] = acc[:, tb - 1:tb]


def _cumsum_heads(logf, tb=256):
    B, T, _ = logf.shape
    tb = min(tb, T)
    return pl.pallas_call(
        functools.partial(_cumsum_kernel, tb),
        grid=(B, T // tb),
        in_specs=[pl.BlockSpec((None, tb, LANES), lambda b, i: (b, i, 0))],
        out_specs=pl.BlockSpec((None, LANES, tb), lambda b, i: (b, 0, i)),
        out_shape=jax.ShapeDtypeStruct((B, LANES, T), F32),
        scratch_shapes=[pltpu.VMEM((LANES, 1), F32)],
        compiler_params=_cparams(("parallel", "arbitrary")),
        name="fox_cumsum",
    )(logf)


def _cumsum_paged_kernel(n_pg, page, H, pt_ref, *refs):
    pages = refs[:n_pg]
    o_ref, carry_ref = refs[n_pg:]

    @pl.when(pl.program_id(1) == 0)
    def _():
        carry_ref[...] = jnp.zeros_like(carry_ref)

    lower = _lower_ones(page)
    widen = (lax.broadcasted_iota(jnp.int32, (H, LANES), 0) == lax.broadcasted_iota(jnp.int32, (H, LANES), 1))
    widen = jnp.where(widen, 1.0, 0.0).astype(BF16)
    carry = carry_ref[...]
    for p, pg in enumerate(pages):
        acc = jnp.zeros((page, LANES), F32)
        for part in _split3(pg[...]):
            wide = jnp.dot(part, widen, preferred_element_type=F32).astype(BF16)
            acc = acc + jnp.dot(lower, wide, preferred_element_type=F32)
        cum = acc.T + carry
        o_ref[:, p * page:(p + 1) * page] = cum
        carry = cum[:, page - 1:page]
    carry_ref[...] = carry


def _cumsum_paged(pool, layer, page_table):
    page, H = pool.shape[2], pool.shape[3]
    B, n_pages = page_table.shape
    n_pg = LOGF_PAGES
    assert n_pages % n_pg == 0 and page % LANES == 0
    page_spec = lambda p: pl.BlockSpec(
        (None, None, page, H), lambda b, c, pt, p=p: (layer, pt[b, c * n_pg + p], 0, 0))
    return pl.pallas_call(
        functools.partial(_cumsum_paged_kernel, n_pg, page, H),
        grid_spec=pltpu.PrefetchScalarGridSpec(
            num_scalar_prefetch=1,
            grid=(B, n_pages // n_pg),
            in_specs=[page_spec(p) for p in range(n_pg)],
            out_specs=pl.BlockSpec((None, LANES, n_pg * page), lambda b, c, pt: (b, 0, c)),
            scratch_shapes=[pltpu.VMEM((LANES, 1), F32)],
        ),
        out_shape=jax.ShapeDtypeStruct((B, LANES, n_pages * page), F32),
        compiler_params=_cparams(("parallel", "arbitrary")),
        name="fox_cumsum_paged",
    )(page_table, *([pool] * n_pg))


def _fox_prompt_kernel(tq, tk, qi, q_ref, k_ref, v_ref, cum_ref, zs_ref, *rest):
    o_ref = rest[-1]
    t0 = qi * tq
    q = q_ref[...]
    qpos = t0 + lax.broadcasted_iota(jnp.int32, (tq, 1), 0)

    def step(kt, carry, masked):
        m, l, acc = carry
        k0 = kt * tk
        v = v_ref[k0:k0 + tk, :]
        s = (lax.dot_general(q, k_ref[k0:k0 + tk, :], NT, preferred_element_type=F32)
             - cum_ref[:, k0:k0 + tk])
        if masked:
            ok = (k0 + lax.broadcasted_iota(jnp.int32, (1, tk), 1)) <= qpos
            s = jnp.where(ok, s, NEG_BIG)
        m_new = jnp.maximum(m, jnp.max(s, axis=-1, keepdims=True))
        alpha = jnp.exp(m - m_new)
        p = jnp.exp(s - m_new)
        if masked:
            p = jnp.where(ok, p, 0.0)
        l = alpha * l + jnp.sum(p, axis=-1, keepdims=True)
        acc = alpha * acc + jnp.dot(p.astype(BF16), v, preferred_element_type=F32)
        return m_new, l, acc

    init = (jnp.full((tq, 1), NEG_BIG, F32), jnp.zeros((tq, 1), F32), jnp.zeros((tq, HEAD_DIM), F32))
    n_full = t0 // tk
    carry = init
    for kt in range(n_full):
        carry = step(kt, carry, False)
    for d in range(tq // tk):
        carry = step(n_full + d, carry, True)
    _, l, acc = carry
    o_ref[...] = (acc / jnp.maximum(l, 1e-30) * zs_ref[...].astype(F32)).astype(o_ref.dtype)


def _fox_prompt_attn(q_bf, kv_bf, cum, zs, tq=512, tk=512):
    B, T, W = q_bf.shape
    H = N_HEADS
    tq, tk = min(tq, T), min(tk, T)
    assert T % tq == 0 and tq % tk == 0
    out = None
    for qi in range(T // tq):
        n_keys = (qi + 1) * tq
        tile = pl.BlockSpec((None, tq, HEAD_DIM), lambda b, h, qi=qi: (b, qi, h))
        in_specs = [
            tile,
            pl.BlockSpec((None, n_keys, HEAD_DIM), lambda b, h: (b, 0, h)),
            pl.BlockSpec((None, n_keys, HEAD_DIM), lambda b, h: (b, 0, H + h)),
            pl.BlockSpec((None, None, 1, n_keys), lambda b, h: (b, h, 0, 0)),
            tile,
        ]
        args = [q_bf, kv_bf, kv_bf, cum, zs]
        aliases = {}
        if out is not None:
            in_specs.append(pl.BlockSpec(memory_space=pl.ANY))
            args.append(out)
            aliases = {len(args) - 1: 0}
        out = pl.pallas_call(
            functools.partial(_fox_prompt_kernel, tq, tk, qi),
            grid=(B, H),
            in_specs=in_specs,
            out_specs=tile,
            out_shape=jax.ShapeDtypeStruct((B, T, W), BF16),
            input_output_aliases=aliases,
            compiler_params=_cparams(("parallel", "parallel")),
            name="fox_prompt_attn",
        )(*args)
    return out


def _fox_sample_kernel(n_pg, page, T, pt_ref, q_ref, *refs):
    blocks = refs[:4 * n_pg]
    cum_ref, kvn_ref, lfn_ref, zs_ref, o_ref, m_ref, l_ref, acc_ref = refs[4 * n_pg:]
    c = pl.program_id(1)
    H = N_HEADS
    n_half = H // SUBLANES

    def head_rows(kv, h):
        return jnp.concatenate(
            [_tile_row(blocks[(p * 2 + kv) * n_half + h // SUBLANES], h % SUBLANES) for p in range(n_pg)], axis=0)

    q = q_ref[...].astype(BF16)

    @pl.when(c == 0)
    def _():
        m_ref[...] = jnp.full(m_ref.shape, NEG_BIG, F32)
        l_ref[...] = jnp.zeros(l_ref.shape, F32)
        acc_ref[...] = jnp.zeros(acc_ref.shape, F32)

    def attend(ks, vs, cum_rows, ok):
        n = ks[0].shape[0]
        s = jnp.concatenate(
            [lax.dot_general(q[:, h * HEAD_DIM:(h + 1) * HEAD_DIM], ks[h], NT, preferred_element_type=F32)
             - jnp.broadcast_to(cum_rows[h], (T, n)) for h in range(H)], axis=0)
        if ok is not None:
            ok = jnp.concatenate([ok] * H, axis=0)
            s = jnp.where(ok, s, NEG_BIG)
        m_new = jnp.maximum(m_ref[:, 0:1], jnp.max(s, axis=-1, keepdims=True))
        alpha = jnp.exp(m_ref[:, 0:1] - m_new)
        p = jnp.exp(s - m_new)
        if ok is not None:
            p = jnp.where(ok, p, 0.0)
        l_new = alpha * l_ref[:, 0:1] + jnp.sum(p, axis=-1, keepdims=True)
        pv = jnp.concatenate(
            [jnp.dot(p[h * T:(h + 1) * T].astype(BF16), vs[h], preferred_element_type=F32) for h in range(H)], axis=0)
        acc_ref[...] = alpha * acc_ref[...] + pv
        m_ref[...] = jnp.broadcast_to(m_new, (H * T, LANES))
        l_ref[...] = jnp.broadcast_to(l_new, (H * T, LANES))

    attend([head_rows(0, h).astype(BF16) for h in range(H)], [head_rows(1, h).astype(BF16) for h in range(H)],
           [cum_ref[h:h + 1, :] for h in range(H)], None)

    @pl.when(c == pl.num_programs(1) - 1)
    def _():
        total = cum_ref[:, n_pg * page - 1:n_pg * page]
        pad = jnp.zeros((LANES - T, LANES), F32)
        acc = jnp.zeros((LANES, LANES), F32)
        lower = _lower_ones(LANES)
        for part in _split3(jnp.concatenate([lfn_ref[...], pad], axis=0)):
            acc = acc + jnp.dot(lower, part, preferred_element_type=F32)
        cum_new = acc.T[:H] + total
        kvn = kvn_ref[...]
        zs = zs_ref[...]
        ok_n = lax.broadcasted_iota(jnp.int32, (1, LANES), 1) <= lax.broadcasted_iota(jnp.int32, (T, 1), 0)
        new = lambda j: jnp.concatenate([kvn[:, j * HEAD_DIM:(j + 1) * HEAD_DIM], pad], axis=0).astype(BF16)
        attend([new(h) for h in range(H)], [new(H + h) for h in range(H)],
               [cum_new[h:h + 1, :] for h in range(H)], ok_n)
        o = acc_ref[...] / jnp.maximum(l_ref[:, 0:1], 1e-30)
        o_ref[...] = jnp.concatenate(
            [o[h * T:(h + 1) * T] * zs[:, h * HEAD_DIM:(h + 1) * HEAD_DIM] for h in range(H)], axis=1)


def _fox_sample_attn(q, kv_pool, layer, page_table, cum, kv_new, logf_new, zs, B, T):
    page = kv_pool.shape[2]
    n_pages = page_table.shape[1]
    n_pg = FOX_PAGES
    H = N_HEADS
    W = H * HEAD_DIM
    assert n_pages % n_pg == 0 and T % SUBLANES == 0 and T <= LANES and H % SUBLANES == 0
    n_half = H // SUBLANES
    block_spec = lambda p, kv, half: pl.BlockSpec(
        (None, None, page, None, SUBLANES, HEAD_DIM),
        lambda b, c, pt: (layer, pt[b, c * n_pg + p], 0, kv, half, 0))
    block_specs = [block_spec(p, kv, half) for p in range(n_pg) for kv in range(2) for half in range(n_half)]
    tok = lambda width: pl.BlockSpec((T, width), lambda b, c, pt: (b, 0))
    return pl.pallas_call(
        functools.partial(_fox_sample_kernel, n_pg, page, T),
        grid_spec=pltpu.PrefetchScalarGridSpec(
            num_scalar_prefetch=1,
            grid=(B, n_pages // n_pg),
            in_specs=[tok(W)] + block_specs + [
                pl.BlockSpec((None, H, n_pg * page), lambda b, c, pt: (b, 0, c)),
                tok(2 * W), tok(LANES), tok(W),
            ],
            out_specs=tok(W),
            scratch_shapes=[
                pltpu.VMEM((H * T, LANES), F32), pltpu.VMEM((H * T, LANES), F32), pltpu.VMEM((H * T, HEAD_DIM), F32),
            ],
        ),
        out_shape=jax.ShapeDtypeStruct((B * T, W), F32),
        compiler_params=_cparams(("parallel", "arbitrary")),
        name="fox_sample_attn",
    )(page_table, q, *([kv_pool] * len(block_specs)), cum, kv_new, logf_new, zs)


def _alibi_slopes():
    s = np.exp2(-8.0 * np.arange(1, N_HEADS + 1) / N_HEADS)
    return jnp.asarray(s, dtype=F32).reshape(A_KV_GROUPS, A_REP)


_NSA_PLAN = (
    (0, 4, "normq", 0, (0,)),
    (4, 6, "raw", 0, (1,)),
    (6, 7, "norm", 2, (2, 4)),
    (7, 8, "raw", 0, (2, 4)),
    (8, 9, "norm", 3, (3, 4)),
    (9, 10, "raw", 0, (3, 4)),
    (10, 14, "silu", 0, (5,)),
)


def _nsa_outs(act_dtype):
    return ((0, 4, act_dtype), (4, 2, F32), (6, 2, F32), (8, 2, F32), (6, 4, act_dtype), (10, 4, act_dtype))


def _nsa_layer(xp, xs, norm_g, w_in, q_gain, k_gain, pe, w1, w2, w_out, slopes, past):
    mix = N_HEADS * HEAD_DIM
    kvw = A_KV_GROUPS * HEAD_DIM
    g0 = mix + 6 * kvw
    g1 = g0 + 3 * N_HEADS
    w_main = jnp.concatenate([w_in[:, :g0], w_in[:, g1:]], axis=1).astype(BF16)
    w_small = jnp.pad(w_in[:, g0:g1], ((0, 0), (0, LANES - 3 * N_HEADS))).astype(BF16)
    gains = jnp.concatenate([q_gain[None], k_gain, jnp.zeros((4, HEAD_DIM), F32)], axis=0)
    zero_bias = jnp.zeros((1, LANES), F32)
    w_out_bf = w_out.astype(BF16)
    slope_tab = jnp.broadcast_to(
        jnp.pad(slopes, ((0, 0), (0, 8 - A_REP)))[:, :, None], (A_KV_GROUPS, 8, LANES))
    shp = lambda B: (B, -1, 2, A_KV_GROUPS, HEAD_DIM)

    B, T, D = xp.shape
    M = B * T
    x2d = xp.reshape(M, D)
    q_bf, cmp, sel, win, rows_bf, zs, gates = _proj(x2d, norm_g, gains, zero_bias, w_main, w_small, _NSA_PLAN,
                                                    _nsa_outs(BF16), "sigmoid", PROJ_ROWS)
    kvc = _compress_prompt(cmp.reshape(B, T, -1), pe, w1.astype(BF16), w2.astype(BF16), k_gain[0])
    gates_g = gates[:, :3 * N_HEADS].reshape(B, T, 3, A_KV_GROUPS, A_REP)
    gates_g = jnp.transpose(gates_g, (0, 3, 1, 2, 4)).reshape(B, A_KV_GROUPS, T, 3 * A_REP)
    n_cmp = T // CMP_BLOCK
    expand = (jnp.arange(T, dtype=jnp.int32)[None, :] // SEL_BLOCK
              == jnp.arange(n_cmp, dtype=jnp.int32)[:, None]).astype(BF16)
    u = _nsa_prompt_attn(q_bf.reshape(B, T, -1), kvc, rows_bf.reshape(B, T, -1), gates_g,
                         zs.reshape(B, T, -1), slope_tab, expand)
    yp = _outproj(u.reshape(M, -1), w_out_bf, x2d, 512).reshape(B, T, D)
    out_p = (cmp.reshape(shp(B)), sel.reshape(shp(B)),
             win.reshape(B, T, -1)[:, T - min(WINDOW, T):].reshape(shp(B)))

    cmp_pool, sel_pool, layer, page_table, win_buf = past
    B, T, D = xs.shape
    M = B * T
    x2d = xs.reshape(M, D)
    q, cmp, sel, win, _, zs, gates = _proj(x2d, norm_g, gains, zero_bias, w_main, w_small, _NSA_PLAN,
                                           _nsa_outs(F32), "sigmoid", M)
    kvc = _compress_paged(cmp_pool, layer, page_table, pe, w1, w2, k_gain[0])
    rows = jnp.concatenate([cmp, sel, win], axis=1)
    u = _nsa_sample_attn(q, kvc, sel_pool, layer, page_table, rows, win_buf, gates, zs, slope_tab, B, T)
    ys = _outproj(u, w_out_bf, x2d, M).reshape(B, T, D)
    out_s = (cmp.reshape(shp(B)), sel.reshape(shp(B)),
             jnp.concatenate([win_buf[:, T:], win.reshape(shp(B))], axis=1))
    return yp, ys, out_p, out_s


_FOX_PLAN = (
    (0, 4, "normq", 0, (0,)),
    (4, 8, "norm", 1, (1, 2)),
    (8, 12, "raw", 0, (1, 2)),
    (12, 16, "silu", 0, (3,)),
)


def _fox_outs(act_dtype):
    return ((0, 4, act_dtype), (4, 8, F32), (4, 8, act_dtype), (12, 4, act_dtype))


def _fox_layer(xp, xs, norm_g, w_in, f_bias, q_gain, k_gain, w_out, past, n_layers, kv_stack):
    mix = N_HEADS * HEAD_DIM
    w_main = jnp.concatenate([w_in[:, :3 * mix], w_in[:, 3 * mix + N_HEADS:]], axis=1).astype(BF16)
    w_small = jnp.pad(w_in[:, 3 * mix:3 * mix + N_HEADS], ((0, 0), (0, LANES - N_HEADS))).astype(BF16)
    gains = jnp.concatenate([q_gain[None], k_gain[None], jnp.zeros((6, HEAD_DIM), F32)], axis=0)
    sbias = jnp.pad(f_bias, (0, LANES - N_HEADS)).reshape(1, LANES)
    w_out_bf = w_out.astype(BF16)

    B, T, D = xp.shape
    M = B * T
    x2d = xp.reshape(M, D)
    kv_pool, logf_pool, layer, page_table = past
    q_bf, kv_stack, kv_bf, zs, logf = _proj(x2d, norm_g, gains, sbias, w_main, w_small, _FOX_PLAN, _fox_outs(BF16),
                                            "logsig", PROJ_ROWS, stacked=(1, layer, n_layers, kv_stack))
    cum = _cumsum_heads(logf.reshape(B, T, LANES))[:, :N_HEADS].reshape(B, N_HEADS, 1, T)
    u = _fox_prompt_attn(q_bf.reshape(B, T, -1), kv_bf.reshape(B, T, -1), cum, zs.reshape(B, T, -1))
    yp = _outproj(u.reshape(M, -1), w_out_bf, x2d, 512).reshape(B, T, D)
    out_p = (kv_stack, logf[:, :N_HEADS].reshape(B, T, N_HEADS))

    B, T, D = xs.shape
    M = B * T
    x2d = xs.reshape(M, D)
    q, kv, _, zs, logf = _proj(x2d, norm_g, gains, sbias, w_main, w_small, _FOX_PLAN, _fox_outs(F32),
                               "logsig", M)
    cum = _cumsum_paged(logf_pool, layer, page_table)
    u = _fox_sample_attn(q, kv_pool, layer, page_table, cum, kv, logf, zs, B, T)
    ys = _outproj(u, w_out_bf, x2d, M).reshape(B, T, D)
    out_s = (kv.reshape(B, T, 2, N_HEADS, HEAD_DIM), logf[:, :N_HEADS].reshape(B, T, N_HEADS))
    return yp, ys, out_p, out_s


def kernel(x_prompt, x_sample, cache_a_cmp, cache_a_sel, state_a_win, cache_b_kv, cache_b_logf, page_table,
           a_norm, a_w_in, a_q_gain, a_k_gain, a_phi_pe, a_phi_w1, a_phi_w2, a_w_out,
           b_norm, b_w_in, b_f_bias, b_q_gain, b_k_gain, b_w_out):
    slopes = _alibi_slopes()
    xp, xs = x_prompt, x_sample
    depth = a_norm.shape[0] + b_norm.shape[0]
    a_p, a_s, b_p, b_s = [], [], [], []
    for i in range(depth):
        j = i // 2
        if i % 2 == 0:
            xp, xs, out_p, out_s = _nsa_layer(
                xp, xs, a_norm[j], a_w_in[j], a_q_gain[j], a_k_gain[j], a_phi_pe[j], a_phi_w1[j], a_phi_w2[j],
                a_w_out[j], slopes, (cache_a_cmp, cache_a_sel, j, page_table, state_a_win[j]))
            a_p.append(out_p)
            a_s.append(out_s)
        else:
            xp, xs, out_p, out_s = _fox_layer(
                xp, xs, b_norm[j], b_w_in[j], b_f_bias[j], b_q_gain[j], b_k_gain[j], b_w_out[j],
                (cache_b_kv, cache_b_logf, j, page_table), b_norm.shape[0], b_p[-1][0] if b_p else None)
            b_p.append(out_p)
            b_s.append(out_s)
    stack = lambda outs, k: jnp.stack([o[k] for o in outs])
    b_kv_p = b_p[-1][0].reshape((b_norm.shape[0],) + x_prompt.shape[:2] + (2, N_HEADS, HEAD_DIM))
    return (xp, xs,
            stack(a_p, 0), stack(a_s, 0), stack(a_p, 1), stack(a_s, 1), stack(a_p, 2), stack(a_s, 2),
            b_kv_p, stack(b_s, 0), stack(b_p, 1), stack(b_s, 1))
```
